```python
import jax, jax.numpy as jnp
from jax import lax
import numpy as np

D_MODEL = 1024
BATCH = 16
SEQ = 2048
DEPTH = 1

D_MIX = D_MODEL
D_ATTN = D_MIX // 2
N_ATTN_HEADS = 8
HEAD_DIM = D_ATTN // N_ATTN_HEADS
D_CONV = D_MIX - D_ATTN
CONV_WIDTH = 31
Q_BLOCK = 128
D_IN_PROJ = 3 * D_ATTN + N_ATTN_HEADS + 2 * D_CONV
N_KEYS = 128
N_EXPERTS = N_KEYS * N_KEYS
PEER_HEADS = 8
PEER_QUERY_DIM = 256
PEER_HALF = PEER_QUERY_DIM // 2
PEER_TOPK = 16
PEER_CHUNK = 128
D_PLE = 256
EPS = 1e-6
LN_EPS = 1e-5

kernel_name = "hymba_fox_conformer_peer_layer"


def rmsnorm(x, g, eps=EPS):
    xf = x.astype(jnp.float32)
    y = xf * lax.rsqrt(jnp.mean(xf * xf, axis=-1, keepdims=True) + eps)
    return (y * g.astype(jnp.float32)).astype(x.dtype)


def layernorm(x, g, b, eps=LN_EPS):
    xf = x.astype(jnp.float32)
    mu = jnp.mean(xf, axis=-1, keepdims=True)
    var = jnp.mean(jnp.square(xf - mu), axis=-1, keepdims=True)
    y = (xf - mu) * lax.rsqrt(var + eps)
    return (y * g.astype(jnp.float32) + b.astype(jnp.float32)).astype(x.dtype)


def forgetting_attention(q, k, v, logf):
    b, s, h, dh = q.shape
    scale = dh ** -0.5
    qh = q.transpose(0, 2, 1, 3)
    kh = k.transpose(0, 2, 1, 3)
    vh = v.transpose(0, 2, 1, 3)
    F = jnp.cumsum(logf.astype(jnp.float32), axis=1).transpose(0, 2, 1)
    outs = []
    for blk in range(s // Q_BLOCK):
        q0 = blk * Q_BLOCK
        end = q0 + Q_BLOCK
        qb = qh[:, :, q0:end]
        kb = kh[:, :, :end]
        vb = vh[:, :, :end]
        logits = jnp.einsum('bhqd,bhkd->bhqk', qb, kb).astype(jnp.float32) * scale
        logits = logits + F[:, :, q0:end, None] - F[:, :, None, :end]
        qpos = q0 + jnp.arange(Q_BLOCK)
        kpos = jnp.arange(end)
        mask = qpos[:, None] >= kpos[None, :]
        logits = jnp.where(mask[None, None], logits, jnp.finfo(jnp.float32).min)
        probs = jax.nn.softmax(logits, axis=-1)
        outs.append(jnp.einsum('bhqk,bhkd->bhqd', probs.astype(vb.dtype), vb))
    o = jnp.concatenate(outs, axis=2)
    return o.transpose(0, 2, 1, 3).reshape(b, s, h * dh)


def conformer_conv(a, g, conv_w, conv_b, ln_g, ln_b):
    u = a * jax.nn.sigmoid(g)
    y = lax.conv_general_dilated(
        u, conv_w.astype(u.dtype), window_strides=(1,), padding=[(CONV_WIDTH - 1, 0)],
        dimension_numbers=('NWC', 'WIO', 'NWC'), feature_group_count=D_CONV)
    y = y + conv_b.astype(y.dtype)
    y = layernorm(y, ln_g, ln_b)
    return jax.nn.silu(y)


def peer_ffn(xn, w_q, sub_keys1, sub_keys2, expert_u, expert_v):
    b, s, d = xn.shape
    qp = (xn @ w_q).reshape(b, s, PEER_HEADS, PEER_QUERY_DIM)
    q1, q2 = qp[..., :PEER_HALF], qp[..., PEER_HALF:]
    s1 = jnp.einsum('bshd,kd->bshk', q1, sub_keys1).astype(jnp.float32)
    s2 = jnp.einsum('bshd,kd->bshk', q2, sub_keys2).astype(jnp.float32)
    v1, i1 = lax.top_k(s1, PEER_TOPK)
    v2, i2 = lax.top_k(s2, PEER_TOPK)
    cand = (v1[..., :, None] + v2[..., None, :]).reshape(b, s, PEER_HEADS, PEER_TOPK * PEER_TOPK)
    vs, ic = lax.top_k(cand, PEER_TOPK)
    e1 = jnp.take_along_axis(i1, ic // PEER_TOPK, axis=-1)
    e2 = jnp.take_along_axis(i2, ic % PEER_TOPK, axis=-1)
    idx = e1 * N_KEYS + e2
    gates = jax.nn.softmax(vs, axis=-1)
    t = b * s
    n_sel = PEER_HEADS * PEER_TOPK
    xt = xn.reshape(t // PEER_CHUNK, PEER_CHUNK, d)
    it = idx.reshape(t // PEER_CHUNK, PEER_CHUNK, n_sel)
    gt = gates.reshape(t // PEER_CHUNK, PEER_CHUNK, n_sel).astype(xn.dtype)

    def chunk_fn(args):
        xc, ic_, gc = args
        act = jax.nn.gelu(jnp.einsum('ckd,cd->ck', expert_u[ic_], xc))
        return jnp.einsum('ck,ckd->cd', gc * act, expert_v[ic_])

    out = lax.map(chunk_fn, (xt, it, gt))
    return out.reshape(b, s, d)


def setup_inputs(seed: int = 0) -> dict:
    key = jax.random.key(seed)
    ks = jax.random.split(key, 24)
    f32 = jnp.float32
    nrm = lambda k, shape, sc: jax.random.normal(k, shape, f32) * sc
    gain = lambda k, shape: 1.0 + 0.05 * jax.random.normal(k, shape, f32)
    L = DEPTH
    return {
        "x": nrm(ks[0], (BATCH, SEQ, D_MODEL), 1.0),
        "p": nrm(ks[1], (DEPTH, BATCH, SEQ, D_PLE), 1.0),
        "ln_mix": gain(ks[2], (L, D_MODEL)),
        "w_in": nrm(ks[3], (L, D_MODEL, D_IN_PROJ), D_MODEL ** -0.5),
        "b_fgate": 2.0 + 0.5 * jax.random.normal(ks[4], (L, N_ATTN_HEADS), f32),
        "q_norm": gain(ks[5], (L, HEAD_DIM)),
        "k_norm": gain(ks[6], (L, HEAD_DIM)),
        "conv_w": nrm(ks[7], (L, CONV_WIDTH, 1, D_CONV), CONV_WIDTH ** -0.5),
        "conv_b": nrm(ks[8], (L, D_CONV), 0.02),
        "conv_ln_g": gain(ks[9], (L, D_CONV)),
        "conv_ln_b": nrm(ks[10], (L, D_CONV), 0.02),
        "attn_out_norm": gain(ks[11], (L, D_ATTN)),
        "conv_out_norm": gain(ks[12], (L, D_CONV)),
        "w_out": nrm(ks[13], (L, D_MIX, D_MODEL), D_MIX ** -0.5),
        "ln_ffn": gain(ks[14], (L, D_MODEL)),
        "w_peer_q": nrm(ks[15], (L, D_MODEL, PEER_HEADS * PEER_QUERY_DIM), D_MODEL ** -0.5),
        "sub_keys1": nrm(ks[16], (L, N_KEYS, PEER_HALF), PEER_HALF ** -0.5),
        "sub_keys2": nrm(ks[17], (L, N_KEYS, PEER_HALF), PEER_HALF ** -0.5),
        "expert_u": nrm(ks[18], (L, N_EXPERTS, D_MODEL), D_MODEL ** -0.5),
        "expert_v": nrm(ks[19], (L, N_EXPERTS, D_MODEL), PEER_HEADS ** -0.5),
        "ln_pl": gain(ks[20], (L, D_MODEL)),
        "w_pl_gate": nrm(ks[21], (L, D_MODEL, D_MODEL), D_MODEL ** -0.5),
        "w_pl_proj": nrm(ks[22], (L, D_PLE, D_MODEL), D_PLE ** -0.5),
    }


def reference(x, p, ln_mix, w_in, b_fgate, q_norm, k_norm, conv_w, conv_b, conv_ln_g, conv_ln_b,
              attn_out_norm, conv_out_norm, w_out, ln_ffn, w_peer_q, sub_keys1, sub_keys2,
              expert_u, expert_v, ln_pl, w_pl_gate, w_pl_proj):
    b, s, _ = x.shape
    h = x
    for i in range(DEPTH):
        xn = rmsnorm(h, ln_mix[i])
        z = xn @ w_in[i]
        o = 0
        q = z[..., o:o + D_ATTN]; o += D_ATTN
        k = z[..., o:o + D_ATTN]; o += D_ATTN
        v = z[..., o:o + D_ATTN]; o += D_ATTN
        f_logit = z[..., o:o + N_ATTN_HEADS]; o += N_ATTN_HEADS
        glu_a = z[..., o:o + D_CONV]; o += D_CONV
        glu_g = z[..., o:o + D_CONV]
        q = rmsnorm(q.reshape(b, s, N_ATTN_HEADS, HEAD_DIM), q_norm[i])
        k = rmsnorm(k.reshape(b, s, N_ATTN_HEADS, HEAD_DIM), k_norm[i])
        v = v.reshape(b, s, N_ATTN_HEADS, HEAD_DIM)
        logf = jax.nn.log_sigmoid(f_logit.astype(jnp.float32) + b_fgate[i].astype(jnp.float32))
        y_attn = forgetting_attention(q, k, v, logf)
        y_conv = conformer_conv(glu_a, glu_g, conv_w[i], conv_b[i], conv_ln_g[i], conv_ln_b[i])
        y = jnp.concatenate([rmsnorm(y_attn, attn_out_norm[i]),
                             rmsnorm(y_conv, conv_out_norm[i])], axis=-1)
        h = h + y @ w_out[i]
        h = h + peer_ffn(rmsnorm(h, ln_ffn[i]), w_peer_q[i], sub_keys1[i], sub_keys2[i],
                         expert_u[i], expert_v[i])
        gate = jax.nn.sigmoid(rmsnorm(h, ln_pl[i]) @ w_pl_gate[i])
        h = h + gate * (p[i] @ w_pl_proj[i])
    return h
```

```python
import functools

import jax
import jax.numpy as jnp
from jax import lax
from jax.experimental import pallas as pl
from jax.experimental.pallas import tpu as pltpu

D_MODEL = 1024
D_ATTN = 512
N_HEADS = 8
HEAD_DIM = 64
D_CONV = 512
CONV_WIDTH = 31
Q_BLOCK = 128
N_KEYS = 128
PEER_HEADS = 8
PEER_QUERY_DIM = 256
PEER_HALF = 128
PEER_TOPK = 16
PEER_CHUNK = 128
D_PLE = 256
EPS = 1e-6
LN_EPS = 1e-5


def _rmsnorm(x, g, eps=EPS):
    y = x * lax.rsqrt(jnp.mean(x * x, axis=-1, keepdims=True) + eps)
    return y * g


def _layernorm(x, g, b, eps=LN_EPS):
    mu = jnp.mean(x, axis=-1, keepdims=True)
    var = jnp.mean(jnp.square(x - mu), axis=-1, keepdims=True)
    return (x - mu) * lax.rsqrt(var + eps) * g + b


def _forgetting_attention(q, k, v, logf):
    b, s, h, dh = q.shape
    scale = dh ** -0.5
    qh = q.transpose(0, 2, 1, 3)
    kh = k.transpose(0, 2, 1, 3)
    vh = v.transpose(0, 2, 1, 3)
    F = jnp.cumsum(logf, axis=1).transpose(0, 2, 1)
    outs = []
    for blk in range(s // Q_BLOCK):
        q0 = blk * Q_BLOCK
        end = q0 + Q_BLOCK
        logits = jnp.einsum('bhqd,bhkd->bhqk', qh[:, :, q0:end], kh[:, :, :end]) * scale
        logits = logits + F[:, :, q0:end, None] - F[:, :, None, :end]
        mask = (q0 + jnp.arange(Q_BLOCK))[:, None] >= jnp.arange(end)[None, :]
        logits = jnp.where(mask[None, None], logits, jnp.finfo(jnp.float32).min)
        probs = jax.nn.softmax(logits, axis=-1)
        outs.append(jnp.einsum('bhqk,bhkd->bhqd', probs, vh[:, :, :end]))
    o = jnp.concatenate(outs, axis=2)
    return o.transpose(0, 2, 1, 3).reshape(b, s, h * dh)


def _conformer_conv(a, g, conv_w, conv_b, ln_g, ln_b):
    u = a * jax.nn.sigmoid(g)
    y = lax.conv_general_dilated(
        u, conv_w, window_strides=(1,), padding=[(CONV_WIDTH - 1, 0)],
        dimension_numbers=('NWC', 'WIO', 'NWC'), feature_group_count=D_CONV)
    y = y + conv_b
    return jax.nn.silu(_layernorm(y, ln_g, ln_b))


def _peer_ffn(xn, w_q, sub_keys1, sub_keys2, expert_u, expert_v):
    b, s, d = xn.shape
    qp = (xn @ w_q).reshape(b, s, PEER_HEADS, PEER_QUERY_DIM)
    q1, q2 = qp[..., :PEER_HALF], qp[..., PEER_HALF:]
    s1 = jnp.einsum('bshd,kd->bshk', q1, sub_keys1)
    s2 = jnp.einsum('bshd,kd->bshk', q2, sub_keys2)
    v1, i1 = lax.top_k(s1, PEER_TOPK)
    v2, i2 = lax.top_k(s2, PEER_TOPK)
    cand = (v1[..., :, None] + v2[..., None, :]).reshape(b, s, PEER_HEADS, PEER_TOPK * PEER_TOPK)
    vs, ic = lax.top_k(cand, PEER_TOPK)
    e1 = jnp.take_along_axis(i1, ic // PEER_TOPK, axis=-1)
    e2 = jnp.take_along_axis(i2, ic % PEER_TOPK, axis=-1)
    idx = e1 * N_KEYS + e2
    gates = jax.nn.softmax(vs, axis=-1)
    t = b * s
    n_sel = PEER_HEADS * PEER_TOPK
    xt = xn.reshape(t // PEER_CHUNK, PEER_CHUNK, d)
    it = idx.reshape(t // PEER_CHUNK, PEER_CHUNK, n_sel)
    gt = gates.reshape(t // PEER_CHUNK, PEER_CHUNK, n_sel)

    def chunk_fn(args):
        xc, ic_, gc = args
        act = jax.nn.gelu(jnp.einsum('ckd,cd->ck', expert_u[ic_], xc))
        return jnp.einsum('ck,ckd->cd', gc * act, expert_v[ic_])

    return lax.map(chunk_fn, (xt, it, gt)).reshape(b, s, d)


PLE_TILE = 512


def _ple_kernel(h_ref, p_ref, g_ref, wg_ref, wp_ref, o_ref):
    h = h_ref[...]
    hn = _rmsnorm(h, g_ref[...])
    gate = jax.nn.sigmoid(jnp.dot(hn.astype(jnp.bfloat16), wg_ref[...],
                                  preferred_element_type=jnp.float32))
    proj = jnp.dot(p_ref[...].astype(jnp.bfloat16), wp_ref[...],
                   preferred_element_type=jnp.float32)
    o_ref[...] = h + gate * proj


def _ple(h, p, ln_pl, w_gate, w_proj):
    t = h.shape[0]
    return pl.pallas_call(
        _ple_kernel,
        grid=(t // PLE_TILE,),
        in_specs=[
            pl.BlockSpec((PLE_TILE, D_MODEL), lambda i: (i, 0)),
            pl.BlockSpec((PLE_TILE, D_PLE), lambda i: (i, 0)),
            pl.BlockSpec((1, D_MODEL), lambda i: (0, 0)),
            pl.BlockSpec((D_MODEL, D_MODEL), lambda i: (0, 0)),
            pl.BlockSpec((D_PLE, D_MODEL), lambda i: (0, 0)),
        ],
        out_specs=pl.BlockSpec((PLE_TILE, D_MODEL), lambda i: (i, 0)),
        out_shape=jax.ShapeDtypeStruct((t, D_MODEL), jnp.float32),
        compiler_params=pltpu.CompilerParams(dimension_semantics=("arbitrary",)),
        name="ple",
    )(h, p, ln_pl.reshape(1, D_MODEL), w_gate.astype(jnp.bfloat16), w_proj.astype(jnp.bfloat16))


def kernel(x, p, ln_mix, w_in, b_fgate, q_norm, k_norm, conv_w, conv_b, conv_ln_g, conv_ln_b,
           attn_out_norm, conv_out_norm, w_out, ln_ffn, w_peer_q, sub_keys1, sub_keys2,
           expert_u, expert_v, ln_pl, w_pl_gate, w_pl_proj):
    b, s, _ = x.shape
    h = x
    for i in range(ln_mix.shape[0]):
        xn = _rmsnorm(h, ln_mix[i])
        z = xn @ w_in[i]
        o = 0
        q = z[..., o:o + D_ATTN]; o += D_ATTN
        k = z[..., o:o + D_ATTN]; o += D_ATTN
        v = z[..., o:o + D_ATTN]; o += D_ATTN
        f_logit = z[..., o:o + N_HEADS]; o += N_HEADS
        glu_a = z[..., o:o + D_CONV]; o += D_CONV
        glu_g = z[..., o:o + D_CONV]
        q = _rmsnorm(q.reshape(b, s, N_HEADS, HEAD_DIM), q_norm[i])
        k = _rmsnorm(k.reshape(b, s, N_HEADS, HEAD_DIM), k_norm[i])
        v = v.reshape(b, s, N_HEADS, HEAD_DIM)
        logf = jax.nn.log_sigmoid(f_logit + b_fgate[i])
        y_attn = _forgetting_attention(q, k, v, logf)
        y_conv = _conformer_conv(glu_a, glu_g, conv_w[i], conv_b[i], conv_ln_g[i], conv_ln_b[i])
        y = jnp.concatenate([_rmsnorm(y_attn, attn_out_norm[i]),
                             _rmsnorm(y_conv, conv_out_norm[i])], axis=-1)
        h = h + y @ w_out[i]
        h = h + _peer_ffn(_rmsnorm(h, ln_ffn[i]), w_peer_q[i], sub_keys1[i], sub_keys2[i],
                          expert_u[i], expert_v[i])
        h = _ple(h.reshape(b * s, D_MODEL), p[i].reshape(b * s, D_PLE), ln_pl[i],
                 w_pl_gate[i], w_pl_proj[i]).reshape(b, s, D_MODEL)
    return h
```

```python
import functools

import jax
import jax.numpy as jnp
from jax import lax
from jax.experimental import pallas as pl
from jax.experimental.pallas import tpu as pltpu

D_MODEL = 1024
D_ATTN = 512
N_HEADS = 8
HEAD_DIM = 64
D_CONV = 512
CONV_WIDTH = 31
Q_BLOCK = 128
N_KEYS = 128
PEER_HEADS = 8
PEER_QUERY_DIM = 256
PEER_HALF = 128
PEER_TOPK = 16
N_SEL = PEER_HEADS * PEER_TOPK
D_PLE = 256
EPS = 1e-6
LN_EPS = 1e-5

EXP_GROUP = 16
EXP_ROWS = EXP_GROUP * N_SEL
EXP_TILE = 256
PACK_TILE = 512
VMEM_LIMIT_BYTES = 56 * 1024 * 1024


def _rmsnorm(x, g, eps=EPS):
    y = x * lax.rsqrt(jnp.mean(x * x, axis=-1, keepdims=True) + eps)
    return y * g


def _layernorm(x, g, b, eps=LN_EPS):
    mu = jnp.mean(x, axis=-1, keepdims=True)
    var = jnp.mean(jnp.square(x - mu), axis=-1, keepdims=True)
    return (x - mu) * lax.rsqrt(var + eps) * g + b


def _forgetting_attention(q, k, v, logf):
    b, s, h, dh = q.shape
    scale = dh ** -0.5
    qh = q.transpose(0, 2, 1, 3)
    kh = k.transpose(0, 2, 1, 3)
    vh = v.transpose(0, 2, 1, 3)
    F = jnp.cumsum(logf, axis=1).transpose(0, 2, 1)
    outs = []
    for blk in range(s // Q_BLOCK):
        q0 = blk * Q_BLOCK
        end = q0 + Q_BLOCK
        logits = jnp.einsum('bhqd,bhkd->bhqk', qh[:, :, q0:end], kh[:, :, :end]) * scale
        logits = logits + F[:, :, q0:end, None] - F[:, :, None, :end]
        mask = (q0 + jnp.arange(Q_BLOCK))[:, None] >= jnp.arange(end)[None, :]
        logits = jnp.where(mask[None, None], logits, jnp.finfo(jnp.float32).min)
        probs = jax.nn.softmax(logits, axis=-1)
        outs.append(jnp.einsum('bhqk,bhkd->bhqd', probs, vh[:, :, :end]))
    o = jnp.concatenate(outs, axis=2)
    return o.transpose(0, 2, 1, 3).reshape(b, s, h * dh)


def _conformer_conv(a, g, conv_w, conv_b, ln_g, ln_b):
    u = a * jax.nn.sigmoid(g)
    y = lax.conv_general_dilated(
        u, conv_w, window_strides=(1,), padding=[(CONV_WIDTH - 1, 0)],
        dimension_numbers=('NWC', 'WIO', 'NWC'), feature_group_count=D_CONV)
    y = y + conv_b
    return jax.nn.silu(_layernorm(y, ln_g, ln_b))


def _peer_retrieve(xn, w_q, sub_keys1, sub_keys2):
    t = xn.shape[0]
    qp = (xn @ w_q).reshape(t, PEER_HEADS, PEER_QUERY_DIM)
    q1, q2 = qp[..., :PEER_HALF], qp[..., PEER_HALF:]
    s1 = jnp.einsum('thd,kd->thk', q1, sub_keys1)
    s2 = jnp.einsum('thd,kd->thk', q2, sub_keys2)
    v1, i1 = lax.top_k(s1, PEER_TOPK)
    v2, i2 = lax.top_k(s2, PEER_TOPK)
    cand = (v1[..., :, None] + v2[..., None, :]).reshape(t, PEER_HEADS, PEER_TOPK * PEER_TOPK)
    vs, ic = lax.top_k(cand, PEER_TOPK)
    e1 = jnp.take_along_axis(i1, ic // PEER_TOPK, axis=-1)
    e2 = jnp.take_along_axis(i2, ic % PEER_TOPK, axis=-1)
    idx = e1 * N_KEYS + e2
    gates = jax.nn.softmax(vs, axis=-1)
    return idx.reshape(t, N_SEL), gates.reshape(t, N_SEL)


def _pack_kernel(u_ref, v_ref, o_ref):
    ub = pltpu.bitcast(u_ref[...].astype(jnp.bfloat16).astype(jnp.float32), jnp.uint32)
    vb = pltpu.bitcast(v_ref[...].astype(jnp.bfloat16).astype(jnp.float32), jnp.uint32)
    o_ref[...] = (vb & jnp.uint32(0xFFFF0000)) | (ub >> 16)


def _pack_experts(expert_u, expert_v):
    n, d = expert_u.shape
    spec = pl.BlockSpec((PACK_TILE, d), lambda i: (i, 0))
    return pl.pallas_call(
        _pack_kernel,
        grid=(n // PACK_TILE,),
        in_specs=[spec, spec],
        out_specs=spec,
        out_shape=jax.ShapeDtypeStruct((n, d), jnp.uint32),
        compiler_params=pltpu.CompilerParams(dimension_semantics=("arbitrary",)),
        name="pack_experts",
    )(expert_u, expert_v)


def _expert_kernel(idx_hbm, table_hbm, xn_ref, gates_ref, h_ref, o_ref,
                   idx_smem, rows, idx_sem, row_sems, *, n_groups):
    step = pl.program_id(0)

    idx_copy = pltpu.make_async_copy(idx_hbm.at[step], idx_smem, idx_sem)
    idx_copy.start()
    idx_copy.wait()

    def row_copy(e, r, slot):
        return pltpu.make_async_copy(table_hbm.at[pl.ds(e, 1)], rows.at[slot, pl.ds(r, 1)],
                                     row_sems.at[slot])

    def issue(g, slot):
        base = g * EXP_ROWS

        def body(r, carry):
            row_copy(idx_smem[base + r], r, slot).start()
            return carry

        lax.fori_loop(0, EXP_ROWS, body, 0, unroll=8)

    def wait_rows(slot):
        pltpu.make_async_copy(table_hbm.at[pl.ds(0, EXP_ROWS)], rows.at[slot],
                              row_sems.at[slot]).wait()

    row_tok = lax.broadcasted_iota(jnp.int32, (EXP_GROUP, EXP_ROWS), 1) // N_SEL
    own = row_tok == lax.broadcasted_iota(jnp.int32, (EXP_GROUP, EXP_ROWS), 0)

    def compute(g, slot):
        tok = pl.ds(pl.multiple_of(g * EXP_GROUP, EXP_GROUP), EXP_GROUP)
        w = rows[slot]
        u = pltpu.bitcast(w << 16, jnp.float32).astype(jnp.bfloat16)
        v = pltpu.bitcast(w & jnp.uint32(0xFFFF0000), jnp.float32).astype(jnp.bfloat16)
        xg = xn_ref[tok, :].astype(jnp.bfloat16)
        pre = lax.dot_general(xg, u, (((1,), (1,)), ((), ())),
                              preferred_element_type=jnp.float32)
        gate = jnp.tile(gates_ref[tok, :], (1, EXP_GROUP))
        a = jnp.where(own, jax.nn.gelu(pre) * gate, 0.0).astype(jnp.bfloat16)
        o_ref[tok, :] = h_ref[tok, :] + jnp.dot(a, v, preferred_element_type=jnp.float32)

    issue(0, 0)

    def group_body(g, carry):
        slot = g % 2

        @pl.when(g + 1 < n_groups)
        def _():
            issue(g + 1, 1 - slot)

        wait_rows(slot)
        compute(g, slot)
        return carry

    lax.fori_loop(0, n_groups, group_body, 0)


def _expert_stage(h, xn, idx, gates, table):
    t, d = h.shape
    n_groups = EXP_TILE // EXP_GROUP
    idx_tiles = idx.reshape(t // EXP_TILE, EXP_TILE * N_SEL)
    tok_spec = pl.BlockSpec((EXP_TILE, d), lambda i: (i, 0))
    return pl.pallas_call(
        functools.partial(_expert_kernel, n_groups=n_groups),
        grid=(t // EXP_TILE,),
        in_specs=[
            pl.BlockSpec(memory_space=pl.ANY),
            pl.BlockSpec(memory_space=pl.ANY),
            tok_spec,
            pl.BlockSpec((EXP_TILE, N_SEL), lambda i: (i, 0)),
            tok_spec,
        ],
        out_specs=tok_spec,
        out_shape=jax.ShapeDtypeStruct((t, d), jnp.float32),
        scratch_shapes=[
            pltpu.SMEM((EXP_TILE * N_SEL,), jnp.int32),
            pltpu.VMEM((2, EXP_ROWS, d), jnp.uint32),
            pltpu.SemaphoreType.DMA(()),
            pltpu.SemaphoreType.DMA((2,)),
        ],
        compiler_params=pltpu.CompilerParams(dimension_semantics=("arbitrary",),
                                             vmem_limit_bytes=VMEM_LIMIT_BYTES),
        name="peer_experts",
    )(idx_tiles, table, xn, gates, h)


PLE_TILE = 512


def _ple_kernel(h_ref, p_ref, g_ref, wg_ref, wp_ref, o_ref):
    h = h_ref[...]
    hn = _rmsnorm(h, g_ref[...])
    gate = jax.nn.sigmoid(jnp.dot(hn.astype(jnp.bfloat16), wg_ref[...],
                                  preferred_element_type=jnp.float32))
    proj = jnp.dot(p_ref[...].astype(jnp.bfloat16), wp_ref[...],
                   preferred_element_type=jnp.float32)
    o_ref[...] = h + gate * proj


def _ple(h, p, ln_pl, w_gate, w_proj):
    t = h.shape[0]
    return pl.pallas_call(
        _ple_kernel,
        grid=(t // PLE_TILE,),
        in_specs=[
            pl.BlockSpec((PLE_TILE, D_MODEL), lambda i: (i, 0)),
            pl.BlockSpec((PLE_TILE, D_PLE), lambda i: (i, 0)),
            pl.BlockSpec((1, D_MODEL), lambda i: (0, 0)),
            pl.BlockSpec((D_MODEL, D_MODEL), lambda i: (0, 0)),
            pl.BlockSpec((D_PLE, D_MODEL), lambda i: (0, 0)),
        ],
        out_specs=pl.BlockSpec((PLE_TILE, D_MODEL), lambda i: (i, 0)),
        out_shape=jax.ShapeDtypeStruct((t, D_MODEL), jnp.float32),
        compiler_params=pltpu.CompilerParams(dimension_semantics=("arbitrary",)),
        name="ple",
    )(h, p, ln_pl.reshape(1, D_MODEL), w_gate.astype(jnp.bfloat16), w_proj.astype(jnp.bfloat16))


def kernel(x, p, ln_mix, w_in, b_fgate, q_norm, k_norm, conv_w, conv_b, conv_ln_g, conv_ln_b,
           attn_out_norm, conv_out_norm, w_out, ln_ffn, w_peer_q, sub_keys1, sub_keys2,
           expert_u, expert_v, ln_pl, w_pl_gate, w_pl_proj):
    b, s, _ = x.shape
    t = b * s
    h = x
    for i in range(ln_mix.shape[0]):
        xn = _rmsnorm(h, ln_mix[i])
        z = xn @ w_in[i]
        o = 0
        q = z[..., o:o + D_ATTN]; o += D_ATTN
        k = z[..., o:o + D_ATTN]; o += D_ATTN
        v = z[..., o:o + D_ATTN]; o += D_ATTN
        f_logit = z[..., o:o + N_HEADS]; o += N_HEADS
        glu_a = z[..., o:o + D_CONV]; o += D_CONV
        glu_g = z[..., o:o + D_CONV]
        q = _rmsnorm(q.reshape(b, s, N_HEADS, HEAD_DIM), q_norm[i])
        k = _rmsnorm(k.reshape(b, s, N_HEADS, HEAD_DIM), k_norm[i])
        v = v.reshape(b, s, N_HEADS, HEAD_DIM)
        logf = jax.nn.log_sigmoid(f_logit + b_fgate[i])
        y_attn = _forgetting_attention(q, k, v, logf)
        y_conv = _conformer_conv(glu_a, glu_g, conv_w[i], conv_b[i], conv_ln_g[i], conv_ln_b[i])
        y = jnp.concatenate([_rmsnorm(y_attn, attn_out_norm[i]),
                             _rmsnorm(y_conv, conv_out_norm[i])], axis=-1)
        h = (h + y @ w_out[i]).reshape(t, D_MODEL)
        hn = _rmsnorm(h, ln_ffn[i])
        idx, gates = _peer_retrieve(hn, w_peer_q[i], sub_keys1[i], sub_keys2[i])
        table = _pack_experts(expert_u[i], expert_v[i])
        h = _expert_stage(h, hn, idx, gates, table)
        h = _ple(h, p[i].reshape(t, D_PLE), ln_pl[i], w_pl_gate[i], w_pl_proj[i]).reshape(b, s, D_MODEL)
    return h
```

```python
import functools

import jax
import jax.numpy as jnp
import numpy as np
from jax import lax
from jax.experimental import pallas as pl
from jax.experimental.pallas import tpu as pltpu

D_MODEL = 1024
D_ATTN = 512
N_HEADS = 8
HEAD_DIM = 64
HEAD_PAD = 128
D_CONV = 512
CONV_WIDTH = 31
CONV_HALO = 32
N_KEYS = 128
PEER_HEADS = 8
PEER_HALF = 128
PEER_TOPK = 16
N_SEL = PEER_HEADS * PEER_TOPK
D_PLE = 256
EPS = 1e-6
LN_EPS = 1e-5
NEG_BIG = -1e30

ROW_TILE = 512
ATTN_BLOCK = 256
ATTN_HEADS = 4
CONV_CHUNK = 64
TOPK_TILE = 128
PACK_TILE = 512
EXP_TILE = 128
EXP_RING = 16
EXP_AHEAD = 12
EXP_PIECES = 8
VMEM_LIMIT_BYTES = 56 * 1024 * 1024

_F32 = jnp.float32
_BF16 = jnp.bfloat16


def _params(n_axes=1):
    return pltpu.CompilerParams(dimension_semantics=("arbitrary",) * n_axes,
                                vmem_limit_bytes=VMEM_LIMIT_BYTES)


def _full(shape):
    return pl.BlockSpec(shape, lambda *_: (0,) * len(shape))


def _dot(a, b):
    return jnp.dot(a, b, preferred_element_type=_F32)


def _dot_nt(a, b):
    return lax.dot_general(a, b, (((1,), (1,)), ((), ())), preferred_element_type=_F32)


def _split3(x):
    hi = x.astype(_BF16)
    r1 = x - hi.astype(_F32)
    mid = r1.astype(_BF16)
    lo = (r1 - mid.astype(_F32)).astype(_BF16)
    return hi, mid, lo


def _rmsnorm(x, g, eps=EPS):
    return x * lax.rsqrt(jnp.mean(x * x, axis=-1, keepdims=True) + eps) * g


def _mix_in_kernel(x_ref, g_ref, wq_ref, wk_ref, wv_ref, wf_ref, bf_ref, wa_ref, wg_ref, bd_ref, tri_ref,
                   pq_ref, pk_ref, qg_ref, kg_ref, qone_ref, kone_ref, vone_ref,
                   qa_ref, ka_ref, va_ref, u_ref, carry_ref, *, tiles_per_seq):
    @pl.when(pl.program_id(0) % tiles_per_seq == 0)
    def _():
        carry_ref[...] = jnp.zeros_like(carry_ref)

    xb = _rmsnorm(x_ref[...], g_ref[...]).astype(_BF16)

    def head_norm(z, gain):
        hi, mid, _ = _split3(z * z)
        ms = _dot(hi, bd_ref[...]) + _dot(mid, bd_ref[...])
        return z * lax.rsqrt(ms + EPS) * gain

    qn = head_norm(_dot(xb, wq_ref[...]), qg_ref[...])
    kn = head_norm(_dot(xb, wk_ref[...]), kg_ref[...])

    z = _dot(xb, wf_ref[...]) + bf_ref[...]
    logf = jnp.minimum(z, 0.0) - jnp.log1p(jnp.exp(-jnp.abs(z)))
    fsum = carry_ref[0:1, :]
    for part in _split3(logf):
        fsum = fsum + _dot(tri_ref[...], part)
    carry_ref[...] = jnp.broadcast_to(fsum[-1:, :], carry_ref.shape)

    f_q = qone_ref[...]
    f_k = kone_ref[...]
    for c, part in enumerate(_split3(fsum)):
        f_q = f_q + _dot(part, pq_ref[c])
        f_k = f_k - _dot(part, pk_ref[c])
    qa_ref[...] = (qn + f_q).astype(_BF16)
    ka_ref[...] = (kn + f_k).astype(_BF16)
    va_ref[...] = (_dot(xb, wv_ref[...]) + vone_ref[...]).astype(_BF16)
    u_ref[...] = _dot(xb, wa_ref[...]) * jax.nn.sigmoid(_dot(xb, wg_ref[...]))


def _pad_heads(w):
    lead = w.shape[:-1]
    w = w.reshape(*lead, N_HEADS, HEAD_DIM)
    w = jnp.pad(w, [(0, 0)] * len(lead) + [(0, 0), (0, HEAD_PAD - HEAD_DIM)])
    return w.reshape(*lead, N_HEADS * HEAD_PAD)


def _place(lane0):
    m = np.zeros((3, 128, N_HEADS * HEAD_PAD), np.float32)
    for c in range(3):
        for h in range(N_HEADS):
            m[c, h, h * HEAD_PAD + lane0 + c] = 1.0
    return jnp.asarray(m, _BF16)


def _lane_ones(lanes):
    m = np.zeros((1, N_HEADS * HEAD_PAD), np.float32)
    for h in range(N_HEADS):
        for l in lanes:
            m[0, h * HEAD_PAD + l] = 1.0
    return jnp.asarray(m)


def _mix_in(x, seq, ln_mix, w_in, b_fgate, q_norm, k_norm):
    t = x.shape[0]
    dp = N_HEADS * HEAD_PAD
    o = 0
    wq = _pad_heads(w_in[:, o:o + D_ATTN]).astype(_BF16); o += D_ATTN
    wk = _pad_heads(w_in[:, o:o + D_ATTN]).astype(_BF16); o += D_ATTN
    wv = _pad_heads(w_in[:, o:o + D_ATTN]).astype(_BF16); o += D_ATTN
    wf = jnp.pad(w_in[:, o:o + N_HEADS], ((0, 0), (0, 128 - N_HEADS))).astype(_BF16); o += N_HEADS
    wa = w_in[:, o:o + D_CONV].astype(_BF16); o += D_CONV
    wg = w_in[:, o:o + D_CONV].astype(_BF16)
    bf = jnp.pad(b_fgate, (0, 128 - N_HEADS)).reshape(1, 128)
    head_of = np.arange(dp) // HEAD_PAD
    bd = jnp.asarray((head_of[:, None] == head_of[None, :]) / HEAD_DIM, _BF16)
    tri = jnp.asarray(np.tril(np.ones((ROW_TILE, ROW_TILE), np.float32)), _BF16)
    qg = _pad_heads(jnp.tile(q_norm, N_HEADS) * HEAD_DIM ** -0.5).reshape(1, dp)
    kg = _pad_heads(jnp.tile(k_norm, N_HEADS)).reshape(1, dp)
    row = pl.BlockSpec((ROW_TILE, dp), lambda i: (i, 0))
    return pl.pallas_call(
        functools.partial(_mix_in_kernel, tiles_per_seq=seq // ROW_TILE),
        grid=(t // ROW_TILE,),
        in_specs=[pl.BlockSpec((ROW_TILE, D_MODEL), lambda i: (i, 0)), _full((1, D_MODEL)),
                  _full((D_MODEL, dp)), _full((D_MODEL, dp)), _full((D_MODEL, dp)),
                  _full((D_MODEL, 128)), _full((1, 128)), _full((D_MODEL, D_CONV)), _full((D_MODEL, D_CONV)),
                  _full((dp, dp)), _full((ROW_TILE, ROW_TILE)), _full((3, 128, dp)), _full((3, 128, dp)),
                  _full((1, dp)), _full((1, dp)), _full((1, dp)), _full((1, dp)), _full((1, dp))],
        out_specs=[row, row, row, pl.BlockSpec((ROW_TILE, D_CONV), lambda i: (i, 0))],
        out_shape=[jax.ShapeDtypeStruct((t, dp), _BF16)] * 3 + [jax.ShapeDtypeStruct((t, D_CONV), _F32)],
        scratch_shapes=[pltpu.VMEM((8, 128), _F32)],
        compiler_params=_params(),
        name="mix_in",
    )(x, ln_mix.reshape(1, D_MODEL), wq, wk, wv, wf, bf, wa, wg, bd, tri,
      _place(HEAD_DIM), _place(HEAD_DIM + 3), qg, kg,
      _lane_ones(range(HEAD_DIM + 3, HEAD_DIM + 6)), _lane_ones(range(HEAD_DIM, HEAD_DIM + 3)),
      _lane_ones([HEAD_DIM]))


def _attn_kernel(q_ref, k_ref, v_ref, o_ref):
    qi = pl.program_id(2)
    rows = lax.broadcasted_iota(jnp.int32, (ATTN_BLOCK, ATTN_BLOCK), 0)
    cols = lax.broadcasted_iota(jnp.int32, (ATTN_BLOCK, ATTN_BLOCK), 1)
    heads = [pl.ds(g * HEAD_PAD, HEAD_PAD) for g in range(ATTN_HEADS)]

    def block(kb, carry, diagonal):
        ks = pl.ds(pl.multiple_of(kb * ATTN_BLOCK, ATTN_BLOCK), ATTN_BLOCK)
        out = []
        for hd, (m, acc) in zip(heads, carry):
            s = _dot_nt(q_ref[:, hd], k_ref[ks, hd])
            if diagonal:
                s = jnp.where(rows >= cols, s, NEG_BIG)
            m_new = jnp.maximum(m, jnp.max(s, axis=1, keepdims=True))
            p = jnp.exp(s - m_new)
            out.append((m_new, acc * jnp.exp(m - m_new) + _dot(p.astype(_BF16), v_ref[ks, hd])))
        return tuple(out)

    init = tuple((jnp.full((ATTN_BLOCK, 1), NEG_BIG, _F32), jnp.zeros((ATTN_BLOCK, HEAD_PAD), _F32))
                 for _ in heads)
    carry = lax.fori_loop(0, qi, lambda kb, c: block(kb, c, False), init)
    lane = lax.broadcasted_iota(jnp.int32, (ATTN_BLOCK, HEAD_PAD), 1)
    for hd, (_, acc) in zip(heads, block(qi, carry, True)):
        o_ref[:, hd] = jnp.where(lane < HEAD_DIM, acc / acc[:, HEAD_DIM:HEAD_DIM + 1], 0.0)


def _attention(qa, ka, va, batch, seq):
    dp = N_HEADS * HEAD_PAD
    gw = ATTN_HEADS * HEAD_PAD
    qa, ka, va = (a.reshape(batch, seq, dp) for a in (qa, ka, va))
    qspec = pl.BlockSpec((None, ATTN_BLOCK, gw), lambda b, h, i: (b, i, h))
    kspec = pl.BlockSpec((None, seq, gw), lambda b, h, i: (b, 0, h))
    out = pl.pallas_call(
        _attn_kernel,
        grid=(batch, N_HEADS // ATTN_HEADS, seq // ATTN_BLOCK),
        in_specs=[qspec, kspec, kspec],
        out_specs=qspec,
        out_shape=jax.ShapeDtypeStruct((batch, seq, dp), _F32),
        compiler_params=_params(3),
        name="fox_attention",
    )(qa, ka, va)
    return out.reshape(batch * seq, dp)


def _conv_kernel(u_ref, halo_ref, w_ref, b_ref, lng_ref, lnb_ref, og_ref, o_ref, pad_ref, *, tiles_per_seq):
    first = pl.program_id(0) % tiles_per_seq == 0
    pad_ref[0:CONV_HALO, :] = jnp.where(first, 0.0, halo_ref[...])
    pad_ref[CONV_HALO:, :] = u_ref[...]
    lead = CONV_HALO - (CONV_WIDTH - 1)
    for c in range(ROW_TILE // CONV_CHUNK):
        r0 = c * CONV_CHUNK
        acc = jnp.zeros((CONV_CHUNK, D_CONV), _F32)
        for j in range(CONV_WIDTH):
            acc = acc + w_ref[j:j + 1, :] * pad_ref[r0 + lead + j:r0 + lead + j + CONV_CHUNK, :]
        y = acc + b_ref[...]
        mu = jnp.mean(y, axis=-1, keepdims=True)
        var = jnp.mean(jnp.square(y - mu), axis=-1, keepdims=True)
        y = (y - mu) * lax.rsqrt(var + LN_EPS) * lng_ref[...] + lnb_ref[...]
        y = y * jax.nn.sigmoid(y)
        o_ref[r0:r0 + CONV_CHUNK, :] = _rmsnorm(y, og_ref[...]).astype(_BF16)


def _conv(u, seq, conv_w, conv_b, ln_g, ln_b, out_g):
    t = u.shape[0]
    w = jnp.pad(conv_w.reshape(CONV_WIDTH, D_CONV), ((0, 32 - CONV_WIDTH), (0, 0)))
    per = ROW_TILE // CONV_HALO
    vec = lambda a: a.reshape(1, D_CONV)
    return pl.pallas_call(
        functools.partial(_conv_kernel, tiles_per_seq=seq // ROW_TILE),
        grid=(t // ROW_TILE,),
        in_specs=[pl.BlockSpec((ROW_TILE, D_CONV), lambda i: (i, 0)),
                  pl.BlockSpec((CONV_HALO, D_CONV), lambda i: (jnp.maximum(i * per - 1, 0), 0)),
                  _full((32, D_CONV))] + [_full((1, D_CONV))] * 4,
        out_specs=pl.BlockSpec((ROW_TILE, D_CONV), lambda i: (i, 0)),
        out_shape=jax.ShapeDtypeStruct((t, D_CONV), _BF16),
        scratch_shapes=[pltpu.VMEM((ROW_TILE + CONV_HALO, D_CONV), _F32)],
        compiler_params=_params(),
        name="conformer_conv",
    )(u, u, w, vec(conv_b), vec(ln_g), vec(ln_b), vec(out_g))


def _mix_out_kernel(ya_ref, yc_ref, x_ref, ag_ref, woa_ref, woc_ref, fg_ref, wq_ref, h_ref, hn_ref, qp_ref):
    ya = ya_ref[...]
    ms = jnp.sum(ya * ya, axis=-1, keepdims=True) * (1.0 / D_ATTN)
    yan = (ya * lax.rsqrt(ms + EPS) * ag_ref[...]).astype(_BF16)
    h = x_ref[...] + _dot(yan, woa_ref[...]) + _dot(yc_ref[...], woc_ref[...])
    h_ref[...] = h
    hn = _rmsnorm(h, fg_ref[...]).astype(_BF16)
    hn_ref[...] = hn
    qp_ref[...] = _dot(hn, wq_ref[...]).astype(_BF16)


def _mix_out(y_attn, y_conv, x, attn_out_norm, w_out, ln_ffn, w_peer_q):
    t = x.shape[0]
    dp = N_HEADS * HEAD_PAD
    dq = w_peer_q.shape[1]
    woa = jnp.pad(w_out[:D_ATTN].reshape(N_HEADS, HEAD_DIM, D_MODEL),
                  ((0, 0), (0, HEAD_PAD - HEAD_DIM), (0, 0))).reshape(dp, D_MODEL).astype(_BF16)
    row = lambda d: pl.BlockSpec((ROW_TILE, d), lambda i: (i, 0))
    return pl.pallas_call(
        _mix_out_kernel,
        grid=(t // ROW_TILE,),
        in_specs=[row(dp), row(D_CONV), row(D_MODEL), _full((1, dp)), _full((dp, D_MODEL)),
                  _full((D_CONV, D_MODEL)), _full((1, D_MODEL)), _full((D_MODEL, dq))],
        out_specs=[row(D_MODEL), row(D_MODEL), row(dq)],
        out_shape=[jax.ShapeDtypeStruct((t, D_MODEL), _F32), jax.ShapeDtypeStruct((t, D_MODEL), _BF16),
                   jax.ShapeDtypeStruct((t, dq), _BF16)],
        compiler_params=_params(),
        name="mix_out",
    )(y_attn, y_conv, x, _pad_heads(attn_out_norm).reshape(1, dp), woa, w_out[D_ATTN:].astype(_BF16),
      ln_ffn.reshape(1, D_MODEL), w_peer_q.astype(_BF16))


def _top16(s):
    n = s.shape[0]
    pos_iota = lax.broadcasted_iota(jnp.int32, s.shape, 0)
    vals, poss = [], []
    for _ in range(PEER_TOPK):
        m = jnp.max(s, axis=0, keepdims=True)
        pos = jnp.min(jnp.where(s == m, pos_iota, n), axis=0, keepdims=True)
        s = jnp.where(pos_iota == pos, -jnp.inf, s)
        vals.append(m)
        poss.append(pos)
    return jnp.concatenate(vals, axis=0), jnp.concatenate(poss, axis=0)


def _pick(table, pos):
    row = lax.broadcasted_iota(jnp.int32, table.shape, 0)
    return jnp.max(jnp.where(row == pos, table, -1), axis=0, keepdims=True)


def _retrieve_kernel(qp_ref, k1_ref, k2_ref, idx_ref, gate_ref):
    half = PEER_TOPK // 2
    sub = lax.broadcasted_iota(jnp.int32, (half, TOPK_TILE), 0)
    pos_blocks = [sub, sub + half] + [sub + i * PEER_TOPK for i in range(1, half)] + [(sub + half) * PEER_TOPK]
    pos_iota = jnp.concatenate(pos_blocks, axis=0)
    idx_rows, gate_rows = [], []
    for h in range(PEER_HEADS):
        c0 = h * 2 * PEER_HALF
        v1, i1 = _top16(_dot_nt(k1_ref[...], qp_ref[:, c0:c0 + PEER_HALF]))
        v2, i2 = _top16(_dot_nt(k2_ref[...], qp_ref[:, c0 + PEER_HALF:c0 + 2 * PEER_HALF]))
        cand = jnp.concatenate([v1[0:1, :] + v2] + [v1[i:i + 1, :] + v2[:half, :] for i in range(1, half)]
                               + [v1[half:, :] + v2[0:1, :]], axis=0)
        vs, ids = [], []
        for _ in range(PEER_TOPK):
            m = jnp.max(cand, axis=0, keepdims=True)
            pos = jnp.min(jnp.where(cand == m, pos_iota, PEER_TOPK * PEER_TOPK), axis=0, keepdims=True)
            cand = jnp.where(pos_iota == pos, -jnp.inf, cand)
            vs.append(m)
            ids.append(_pick(i1, pos // PEER_TOPK) * N_KEYS + _pick(i2, pos % PEER_TOPK))
        vs = jnp.concatenate(vs, axis=0)
        e = jnp.exp(vs - vs[0:1, :])
        gate_rows.append(e / jnp.sum(e, axis=0, keepdims=True))
        idx_rows.append(jnp.concatenate(ids, axis=0))
    idx_ref[...] = jnp.concatenate(idx_rows, axis=0).T
    gate_ref[...] = jnp.concatenate(gate_rows, axis=0).T


def _retrieve(qp, sub_keys1, sub_keys2):
    t, dq = qp.shape
    out = pl.BlockSpec((TOPK_TILE, N_SEL), lambda i: (i, 0))
    return pl.pallas_call(
        _retrieve_kernel,
        grid=(t // TOPK_TILE,),
        in_specs=[pl.BlockSpec((TOPK_TILE, dq), lambda i: (i, 0)), _full((N_KEYS, PEER_HALF)),
                  _full((N_KEYS, PEER_HALF))],
        out_specs=[out, out],
        out_shape=[jax.ShapeDtypeStruct((t, N_SEL), jnp.int32), jax.ShapeDtypeStruct((t, N_SEL), _F32)],
        compiler_params=_params(),
        name="peer_retrieve",
    )(qp, sub_keys1.astype(_BF16), sub_keys2.astype(_BF16))


def _pack_kernel(u_ref, v_ref, o_ref):
    d = u_ref.shape[1]
    o_ref[:, 0, :d] = u_ref[...]
    o_ref[:, 0, d:] = v_ref[...]


def _pack_experts(expert_u, expert_v):
    n, d = expert_u.shape
    spec = pl.BlockSpec((PACK_TILE, d), lambda i: (i, 0))
    return pl.pallas_call(
        _pack_kernel,
        grid=(n // PACK_TILE,),
        in_specs=[spec, spec],
        out_specs=pl.BlockSpec((PACK_TILE, 1, 2 * d), lambda i: (i, 0, 0)),
        out_shape=jax.ShapeDtypeStruct((n, 1, 2 * d), _F32),
        compiler_params=_params(),
        name="pack_experts",
    )(expert_u, expert_v)


def _expert_kernel(idx_hbm, table_hbm, xn_ref, gates_ref, h_ref, o_ref,
                   idx_smem, rows, idx_sems, row_sems, *, n_steps):
    step = pl.program_id(0)
    cur = step % 2
    last = step + 1 == n_steps
    tile_idx = EXP_TILE * N_SEL
    d = D_MODEL
    per_piece = N_SEL // EXP_PIECES
    chunk = 2 * d // EXP_PIECES

    def idx_copy(s, half):
        return pltpu.make_async_copy(idx_hbm.at[s], idx_smem.at[pl.ds(half * tile_idx, tile_idx)],
                                     idx_sems.at[half])

    def issue(base, buf, piece):
        for k in range(piece * per_piece, (piece + 1) * per_piece):
            pltpu.make_async_copy(table_hbm.at[idx_smem[base + k]], rows.at[buf, pl.ds(k, 1)],
                                  row_sems.at[buf]).start(priority=k % 2)

    def wait_rows(buf):
        pltpu.make_async_copy(rows.at[buf], rows.at[buf], row_sems.at[buf]).wait()

    own_row = lax.broadcasted_iota(jnp.int32, (EXP_RING, N_SEL), 0)
    half_pieces = EXP_PIECES // 2

    def u_piece(j, c, xg, pre):
        return pre + _dot_nt(xg[:, c * chunk:(c + 1) * chunk],
                             rows[j, :, pl.ds(c * chunk, chunk)].astype(_BF16))

    def v_piece(j, c, a, acc_c):
        return acc_c + _dot(a, rows[j, :, pl.ds(d + c * chunk, chunk)].astype(_BF16))

    @pl.when(step == 0)
    def _():
        first = idx_copy(0, 0)
        first.start()
        first.wait()

        def prologue(j, carry):
            for piece in range(EXP_PIECES):
                issue(j * N_SEL, j, piece)
            return carry

        lax.fori_loop(0, EXP_AHEAD, prologue, 0)

    @pl.when(jnp.logical_not(last))
    def _():
        idx_copy(step + 1, 1 - cur).start()

    next_idx_at = (EXP_RING - EXP_AHEAD % EXP_RING) % EXP_RING

    def ring_body(i, carry):
        t0 = pl.multiple_of(i * EXP_RING, EXP_RING)
        grp = pl.ds(t0, EXP_RING)
        xg = xn_ref[grp, :]
        gates = gates_ref[grp, :]
        h = h_ref[grp, :]
        acc = [h[:, c * chunk:(c + 1) * chunk] for c in range(half_pieces)]
        wait_rows(0)
        pre = jnp.zeros((EXP_RING, N_SEL), _F32)
        for c in range(half_pieces):
            pre = u_piece(0, c, xg, pre)
        for j in range(EXP_RING):
            ta = t0 + j + EXP_AHEAD
            if j == next_idx_at:
                @pl.when(jnp.logical_and(ta == EXP_TILE, jnp.logical_not(last)))
                def _():
                    idx_copy(step + 1, 1 - cur).wait()
            past = ta >= EXP_TILE
            half = jnp.where(jnp.logical_and(past, jnp.logical_not(last)), 1 - cur, cur)
            base_ahead = half * tile_idx + jnp.where(past, ta - EXP_TILE, ta) * N_SEL
            buf_ahead = (j + EXP_AHEAD) % EXP_RING
            more = j + 1 < EXP_RING
            if more:
                wait_rows(j + 1)
                nxt = jnp.zeros((EXP_RING, N_SEL), _F32)
                for c in range(half_pieces):
                    issue(base_ahead, buf_ahead, c)
                    nxt = u_piece(j + 1, c, xg, nxt)
            a = jnp.where(own_row == j, jax.nn.gelu(pre) * gates, 0.0).astype(_BF16)
            for c in range(half_pieces):
                if more:
                    issue(base_ahead, buf_ahead, half_pieces + c)
                else:
                    issue(base_ahead, buf_ahead, 2 * c)
                    issue(base_ahead, buf_ahead, 2 * c + 1)
                acc[c] = v_piece(j, c, a, acc[c])
            if more:
                pre = nxt
        o_ref[grp, :] = jnp.concatenate(acc, axis=1)
        return carry

    lax.fori_loop(0, EXP_TILE // EXP_RING, ring_body, 0)

    @pl.when(last)
    def _():
        for j in range(EXP_AHEAD):
            wait_rows(j)


def _expert_stage(h, hn, idx, gates, table):
    t, d = h.shape
    n_steps = t // EXP_TILE
    tok = pl.BlockSpec((EXP_TILE, d), lambda i: (i, 0))
    return pl.pallas_call(
        functools.partial(_expert_kernel, n_steps=n_steps),
        grid=(n_steps,),
        in_specs=[pl.BlockSpec(memory_space=pl.ANY), pl.BlockSpec(memory_space=pl.ANY), tok,
                  pl.BlockSpec((EXP_TILE, N_SEL), lambda i: (i, 0)), tok],
        out_specs=tok,
        out_shape=jax.ShapeDtypeStruct((t, d), _F32),
        scratch_shapes=[pltpu.SMEM((2 * EXP_TILE * N_SEL,), jnp.int32),
                        pltpu.VMEM((EXP_RING, N_SEL, 2 * d), _F32),
                        pltpu.SemaphoreType.DMA((2,)),
                        pltpu.SemaphoreType.DMA((EXP_RING,))],
        compiler_params=_params(),
        name="peer_experts",
    )(idx.reshape(n_steps, EXP_TILE * N_SEL), table, hn, gates, h)


def _ple_kernel(h_ref, p_ref, g_ref, wg_ref, wp_ref, o_ref):
    h = h_ref[...]
    gate = jax.nn.sigmoid(_dot(_rmsnorm(h, g_ref[...]).astype(_BF16), wg_ref[...]))
    o_ref[...] = h + gate * _dot(p_ref[...].astype(_BF16), wp_ref[...])


def _ple(h, p, ln_pl, w_gate, w_proj):
    t = h.shape[0]
    row = lambda d: pl.BlockSpec((ROW_TILE, d), lambda i: (i, 0))
    return pl.pallas_call(
        _ple_kernel,
        grid=(t // ROW_TILE,),
        in_specs=[row(D_MODEL), row(D_PLE), _full((1, D_MODEL)), _full((D_MODEL, D_MODEL)),
                  _full((D_PLE, D_MODEL))],
        out_specs=row(D_MODEL),
        out_shape=jax.ShapeDtypeStruct((t, D_MODEL), _F32),
        compiler_params=_params(),
        name="ple",
    )(h, p, ln_pl.reshape(1, D_MODEL), w_gate.astype(_BF16), w_proj.astype(_BF16))


def kernel(x, p, ln_mix, w_in, b_fgate, q_norm, k_norm, conv_w, conv_b, conv_ln_g, conv_ln_b,
           attn_out_norm, conv_out_norm, w_out, ln_ffn, w_peer_q, sub_keys1, sub_keys2,
           expert_u, expert_v, ln_pl, w_pl_gate, w_pl_proj):
    b, s, _ = x.shape
    t = b * s
    h = x.reshape(t, D_MODEL)
    for i in range(ln_mix.shape[0]):
        qa, ka, va, u = _mix_in(h, s, ln_mix[i], w_in[i], b_fgate[i], q_norm[i], k_norm[i])
        y_attn = _attention(qa, ka, va, b, s)
        y_conv = _conv(u, s, conv_w[i], conv_b[i], conv_ln_g[i], conv_ln_b[i], conv_out_norm[i])
        h, hn, qp = _mix_out(y_attn, y_conv, h, attn_out_norm[i], w_out[i], ln_ffn[i], w_peer_q[i])
        idx, gates = _retrieve(qp, sub_keys1[i], sub_keys2[i])
        table = _pack_experts(expert_u[i], expert_v[i])
        h = _expert_stage(h, hn, idx, gates, table)
        h = _ple(h, p[i].reshape(t, D_PLE), ln_pl[i], w_pl_gate[i], w_pl_proj[i])
    return h.reshape(b, s, D_MODEL)
```

```python
import functools

import jax
import jax.numpy as jnp
import numpy as np
from jax import lax
from jax.experimental import pallas as pl
from jax.experimental.pallas import tpu as pltpu

D_MODEL = 1024
D_ATTN = 512
N_HEADS = 8
HEAD_DIM = 64
HEAD_PAD = 128
D_CONV = 512
CONV_WIDTH = 31
CONV_HALO = 32
N_KEYS = 128
PEER_HEADS = 8
PEER_HALF = 128
PEER_TOPK = 16
N_SEL = PEER_HEADS * PEER_TOPK
D_PLE = 256
EPS = 1e-6
LN_EPS = 1e-5
NEG_BIG = -1e30

ROW_TILE = 512
ATTN_BLOCK = 256
ATTN_HEADS = 4
CONV_CHUNK = 64
TOPK_TILE = 128
PACK_TILE = 512
EXP_TILE = 128
EXP_RING = 16
EXP_AHEAD = 12
EXP_PIECES = 8
VMEM_LIMIT_BYTES = 56 * 1024 * 1024

_F32 = jnp.float32
_BF16 = jnp.bfloat16


def _params(n_axes=1):
    return pltpu.CompilerParams(dimension_semantics=("arbitrary",) * n_axes,
                                vmem_limit_bytes=VMEM_LIMIT_BYTES)


def _full(shape):
    return pl.BlockSpec(shape, lambda *_: (0,) * len(shape))


def _dot(a, b):
    return jnp.dot(a, b, preferred_element_type=_F32)


def _dot_nt(a, b):
    return lax.dot_general(a, b, (((1,), (1,)), ((), ())), preferred_element_type=_F32)


def _split3(x):
    hi = x.astype(_BF16)
    r1 = x - hi.astype(_F32)
    mid = r1.astype(_BF16)
    lo = (r1 - mid.astype(_F32)).astype(_BF16)
    return hi, mid, lo


def _rmsnorm(x, g, eps=EPS):
    return x * lax.rsqrt(jnp.mean(x * x, axis=-1, keepdims=True) + eps) * g


def _mix_in_kernel(x_ref, g_ref, wq_ref, wk_ref, wv_ref, wf_ref, bf_ref, wa_ref, wg_ref, bd_ref, tri_ref,
                   pq_ref, pk_ref, qg_ref, kg_ref, qone_ref, kone_ref, vone_ref,
                   qa_ref, ka_ref, va_ref, u_ref, carry_ref, *, tiles_per_seq):
    @pl.when(pl.program_id(0) % tiles_per_seq == 0)
    def _():
        carry_ref[...] = jnp.zeros_like(carry_ref)

    xb = _rmsnorm(x_ref[...], g_ref[...]).astype(_BF16)

    def head_norm(z, gain):
        hi, mid, _ = _split3(z * z)
        ms = _dot(hi, bd_ref[...]) + _dot(mid, bd_ref[...])
        return z * lax.rsqrt(ms + EPS) * gain

    qn = head_norm(_dot(xb, wq_ref[...]), qg_ref[...])
    kn = head_norm(_dot(xb, wk_ref[...]), kg_ref[...])

    z = _dot(xb, wf_ref[...]) + bf_ref[...]
    logf = jnp.minimum(z, 0.0) - jnp.log1p(jnp.exp(-jnp.abs(z)))
    fsum = carry_ref[0:1, :]
    for part in _split3(logf):
        fsum = fsum + _dot(tri_ref[...], part)
    carry_ref[...] = jnp.broadcast_to(fsum[-1:, :], carry_ref.shape)

    f_q = qone_ref[...]
    f_k = kone_ref[...]
    for c, part in enumerate(_split3(fsum)):
        f_q = f_q + _dot(part, pq_ref[c])
        f_k = f_k - _dot(part, pk_ref[c])
    qa_ref[...] = (qn + f_q).astype(_BF16)
    ka_ref[...] = (kn + f_k).astype(_BF16)
    va_ref[...] = (_dot(xb, wv_ref[...]) + vone_ref[...]).astype(_BF16)
    u_ref[...] = _dot(xb, wa_ref[...]) * jax.nn.sigmoid(_dot(xb, wg_ref[...]))


def _pad_heads(w):
    lead = w.shape[:-1]
    w = w.reshape(*lead, N_HEADS, HEAD_DIM)
    w = jnp.pad(w, [(0, 0)] * len(lead) + [(0, 0), (0, HEAD_PAD - HEAD_DIM)])
    return w.reshape(*lead, N_HEADS * HEAD_PAD)


def _place(lane0):
    m = np.zeros((3, 128, N_HEADS * HEAD_PAD), np.float32)
    for c in range(3):
        for h in range(N_HEADS):
            m[c, h, h * HEAD_PAD + lane0 + c] = 1.0
    return jnp.asarray(m, _BF16)


def _lane_ones(lanes):
    m = np.zeros((1, N_HEADS * HEAD_PAD), np.float32)
    for h in range(N_HEADS):
        for l in lanes:
            m[0, h * HEAD_PAD + l] = 1.0
    return jnp.asarray(m)


def _mix_in(x, seq, ln_mix, w_in, b_fgate, q_norm, k_norm):
    t = x.shape[0]
    dp = N_HEADS * HEAD_PAD
    o = 0
    wq = _pad_heads(w_in[:, o:o + D_ATTN]).astype(_BF16); o += D_ATTN
    wk = _pad_heads(w_in[:, o:o + D_ATTN]).astype(_BF16); o += D_ATTN
    wv = _pad_heads(w_in[:, o:o + D_ATTN]).astype(_BF16); o += D_ATTN
    wf = jnp.pad(w_in[:, o:o + N_HEADS], ((0, 0), (0, 128 - N_HEADS))).astype(_BF16); o += N_HEADS
    wa = w_in[:, o:o + D_CONV].astype(_BF16); o += D_CONV
    wg = w_in[:, o:o + D_CONV].astype(_BF16)
    bf = jnp.pad(b_fgate, (0, 128 - N_HEADS)).reshape(1, 128)
    head_of = np.arange(dp) // HEAD_PAD
    bd = jnp.asarray((head_of[:, None] == head_of[None, :]) / HEAD_DIM, _BF16)
    tri = jnp.asarray(np.tril(np.ones((ROW_TILE, ROW_TILE), np.float32)), _BF16)
    qg = _pad_heads(jnp.tile(q_norm, N_HEADS) * HEAD_DIM ** -0.5).reshape(1, dp)
    kg = _pad_heads(jnp.tile(k_norm, N_HEADS)).reshape(1, dp)
    row = pl.BlockSpec((ROW_TILE, dp), lambda i: (i, 0))
    return pl.pallas_call(
        functools.partial(_mix_in_kernel, tiles_per_seq=seq // ROW_TILE),
        grid=(t // ROW_TILE,),
        in_specs=[pl.BlockSpec((ROW_TILE, D_MODEL), lambda i: (i, 0)), _full((1, D_MODEL)),
                  _full((D_MODEL, dp)), _full((D_MODEL, dp)), _full((D_MODEL, dp)),
                  _full((D_MODEL, 128)), _full((1, 128)), _full((D_MODEL, D_CONV)), _full((D_MODEL, D_CONV)),
                  _full((dp, dp)), _full((ROW_TILE, ROW_TILE)), _full((3, 128, dp)), _full((3, 128, dp)),
                  _full((1, dp)), _full((1, dp)), _full((1, dp)), _full((1, dp)), _full((1, dp))],
        out_specs=[row, row, row, pl.BlockSpec((ROW_TILE, D_CONV), lambda i: (i, 0))],
        out_shape=[jax.ShapeDtypeStruct((t, dp), _BF16)] * 3 + [jax.ShapeDtypeStruct((t, D_CONV), _F32)],
        scratch_shapes=[pltpu.VMEM((8, 128), _F32)],
        compiler_params=_params(),
        name="mix_in",
    )(x, ln_mix.reshape(1, D_MODEL), wq, wk, wv, wf, bf, wa, wg, bd, tri,
      _place(HEAD_DIM), _place(HEAD_DIM + 3), qg, kg,
      _lane_ones(range(HEAD_DIM + 3, HEAD_DIM + 6)), _lane_ones(range(HEAD_DIM, HEAD_DIM + 3)),
      _lane_ones([HEAD_DIM]))


def _attn_kernel(q_ref, k_ref, v_ref, o_ref):
    qi = pl.program_id(2)
    rows = lax.broadcasted_iota(jnp.int32, (ATTN_BLOCK, ATTN_BLOCK), 0)
    cols = lax.broadcasted_iota(jnp.int32, (ATTN_BLOCK, ATTN_BLOCK), 1)
    heads = [pl.ds(g * HEAD_PAD, HEAD_PAD) for g in range(ATTN_HEADS)]

    def block(kb, carry, diagonal):
        ks = pl.ds(pl.multiple_of(kb * ATTN_BLOCK, ATTN_BLOCK), ATTN_BLOCK)
        out = []
        for hd, (m, acc) in zip(heads, carry):
            s = _dot_nt(q_ref[:, hd], k_ref[ks, hd])
            if diagonal:
                s = jnp.where(rows >= cols, s, NEG_BIG)
            m_new = jnp.maximum(m, jnp.max(s, axis=1, keepdims=True))
            p = jnp.exp(s - m_new)
            out.append((m_new, acc * jnp.exp(m - m_new) + _dot(p.astype(_BF16), v_ref[ks, hd])))
        return tuple(out)

    init = tuple((jnp.full((ATTN_BLOCK, 1), NEG_BIG, _F32), jnp.zeros((ATTN_BLOCK, HEAD_PAD), _F32))
                 for _ in heads)
    carry = lax.fori_loop(0, qi, lambda kb, c: block(kb, c, False), init)
    lane = lax.broadcasted_iota(jnp.int32, (ATTN_BLOCK, HEAD_PAD), 1)
    for hd, (_, acc) in zip(heads, block(qi, carry, True)):
        o_ref[:, hd] = jnp.where(lane < HEAD_DIM, acc / acc[:, HEAD_DIM:HEAD_DIM + 1], 0.0)


def _attention(qa, ka, va, batch, seq):
    dp = N_HEADS * HEAD_PAD
    gw = ATTN_HEADS * HEAD_PAD
    qa, ka, va = (a.reshape(batch, seq, dp) for a in (qa, ka, va))
    qspec = pl.BlockSpec((None, ATTN_BLOCK, gw), lambda b, h, i: (b, i, h))
    kspec = pl.BlockSpec((None, seq, gw), lambda b, h, i: (b, 0, h))
    out = pl.pallas_call(
        _attn_kernel,
        grid=(batch, N_HEADS // ATTN_HEADS, seq // ATTN_BLOCK),
        in_specs=[qspec, kspec, kspec],
        out_specs=qspec,
        out_shape=jax.ShapeDtypeStruct((batch, seq, dp), _F32),
        compiler_params=_params(3),
        name="fox_attention",
    )(qa, ka, va)
    return out.reshape(batch * seq, dp)


def _conv_kernel(u_ref, halo_ref, w_ref, b_ref, lng_ref, lnb_ref, og_ref, o_ref, pad_ref, *, tiles_per_seq):
    first = pl.program_id(0) % tiles_per_seq == 0
    pad_ref[0:CONV_HALO, :] = jnp.where(first, 0.0, halo_ref[...])
    pad_ref[CONV_HALO:, :] = u_ref[...]
    lead = CONV_HALO - (CONV_WIDTH - 1)
    for c in range(ROW_TILE // CONV_CHUNK):
        r0 = c * CONV_CHUNK
        acc = jnp.zeros((CONV_CHUNK, D_CONV), _F32)
        for j in range(CONV_WIDTH):
            acc = acc + w_ref[j:j + 1, :] * pad_ref[r0 + lead + j:r0 + lead + j + CONV_CHUNK, :]
        y = acc + b_ref[...]
        mu = jnp.mean(y, axis=-1, keepdims=True)
        var = jnp.mean(jnp.square(y - mu), axis=-1, keepdims=True)
        y = (y - mu) * lax.rsqrt(var + LN_EPS) * lng_ref[...] + lnb_ref[...]
        y = y * jax.nn.sigmoid(y)
        o_ref[r0:r0 + CONV_CHUNK, :] = _rmsnorm(y, og_ref[...]).astype(_BF16)


def _conv(u, seq, conv_w, conv_b, ln_g, ln_b, out_g):
    t = u.shape[0]
    w = jnp.pad(conv_w.reshape(CONV_WIDTH, D_CONV), ((0, 32 - CONV_WIDTH), (0, 0)))
    per = ROW_TILE // CONV_HALO
    vec = lambda a: a.reshape(1, D_CONV)
    return pl.pallas_call(
        functools.partial(_conv_kernel, tiles_per_seq=seq // ROW_TILE),
        grid=(t // ROW_TILE,),
        in_specs=[pl.BlockSpec((ROW_TILE, D_CONV), lambda i: (i, 0)),
                  pl.BlockSpec((CONV_HALO, D_CONV), lambda i: (jnp.maximum(i * per - 1, 0), 0)),
                  _full((32, D_CONV))] + [_full((1, D_CONV))] * 4,
        out_specs=pl.BlockSpec((ROW_TILE, D_CONV), lambda i: (i, 0)),
        out_shape=jax.ShapeDtypeStruct((t, D_CONV), _BF16),
        scratch_shapes=[pltpu.VMEM((ROW_TILE + CONV_HALO, D_CONV), _F32)],
        compiler_params=_params(),
        name="conformer_conv",
    )(u, u, w, vec(conv_b), vec(ln_g), vec(ln_b), vec(out_g))


def _mix_out_kernel(ya_ref, yc_ref, x_ref, ag_ref, woa_ref, woc_ref, fg_ref, wq_ref, h_ref, hn_ref, qp_ref):
    ya = ya_ref[...]
    ms = jnp.sum(ya * ya, axis=-1, keepdims=True) * (1.0 / D_ATTN)
    yan = (ya * lax.rsqrt(ms + EPS) * ag_ref[...]).astype(_BF16)
    h = x_ref[...] + _dot(yan, woa_ref[...]) + _dot(yc_ref[...], woc_ref[...])
    h_ref[...] = h
    hn = _rmsnorm(h, fg_ref[...]).astype(_BF16)
    hn_ref[...] = hn
    qp_ref[...] = _dot(hn, wq_ref[...]).astype(_BF16)


def _mix_out(y_attn, y_conv, x, attn_out_norm, w_out, ln_ffn, w_peer_q):
    t = x.shape[0]
    dp = N_HEADS * HEAD_PAD
    dq = w_peer_q.shape[1]
    woa = jnp.pad(w_out[:D_ATTN].reshape(N_HEADS, HEAD_DIM, D_MODEL),
                  ((0, 0), (0, HEAD_PAD - HEAD_DIM), (0, 0))).reshape(dp, D_MODEL).astype(_BF16)
    row = lambda d: pl.BlockSpec((ROW_TILE, d), lambda i: (i, 0))
    return pl.pallas_call(
        _mix_out_kernel,
        grid=(t // ROW_TILE,),
        in_specs=[row(dp), row(D_CONV), row(D_MODEL), _full((1, dp)), _full((dp, D_MODEL)),
                  _full((D_CONV, D_MODEL)), _full((1, D_MODEL)), _full((D_MODEL, dq))],
        out_specs=[row(D_MODEL), row(D_MODEL), row(dq)],
        out_shape=[jax.ShapeDtypeStruct((t, D_MODEL), _F32), jax.ShapeDtypeStruct((t, D_MODEL), _BF16),
                   jax.ShapeDtypeStruct((t, dq), _BF16)],
        compiler_params=_params(),
        name="mix_out",
    )(y_attn, y_conv, x, _pad_heads(attn_out_norm).reshape(1, dp), woa, w_out[D_ATTN:].astype(_BF16),
      ln_ffn.reshape(1, D_MODEL), w_peer_q.astype(_BF16))


def _merge_sort_pairs(n):
    pairs = []
    p = 1
    while p < n:
        k = p
        while k >= 1:
            for j in range(k % p, n - k, 2 * k):
                for i in range(min(k, n - j - k)):
                    if (i + j) // (2 * p) == (i + j + k) // (2 * p):
                        pairs.append((i + j, i + j + k))
            k //= 2
        p *= 2
    return pairs


def _top16(s):
    slabs = s.shape[0] // 8
    sub = lax.broadcasted_iota(jnp.int32, (8, s.shape[1]), 0)
    vals = [s[8 * g:8 * g + 8, :] for g in range(slabs)]
    idxs = [sub + 8 * g for g in range(slabs)]
    for a, b in _merge_sort_pairs(slabs):
        first = (vals[a] > vals[b]) | ((vals[a] == vals[b]) & (idxs[a] < idxs[b]))
        vals[a], vals[b] = jnp.where(first, vals[a], vals[b]), jnp.where(first, vals[b], vals[a])
        idxs[a], idxs[b] = jnp.where(first, idxs[a], idxs[b]), jnp.where(first, idxs[b], idxs[a])
    top_v, top_i = [], []
    for it in range(PEER_TOPK):
        m = jnp.max(vals[0], axis=0, keepdims=True)
        pos = jnp.min(jnp.where(vals[0] == m, idxs[0], s.shape[0]), axis=0, keepdims=True)
        won = idxs[0] == pos
        top_v.append(m)
        top_i.append(pos)
        for g in range(PEER_TOPK - 1 - it):
            vals[g] = jnp.where(won, vals[g + 1], vals[g])
            idxs[g] = jnp.where(won, idxs[g + 1], idxs[g])
    return jnp.concatenate(top_v, axis=0), jnp.concatenate(top_i, axis=0)


def _pick(table, pos):
    row = lax.broadcasted_iota(jnp.int32, table.shape, 0)
    return jnp.max(jnp.where(row == pos, table, -1), axis=0, keepdims=True)


def _retrieve_kernel(qp_ref, k1_ref, k2_ref, idx_ref, gate_ref):
    half = PEER_TOPK // 2
    sub = lax.broadcasted_iota(jnp.int32, (half, TOPK_TILE), 0)
    pos_blocks = [sub, sub + half] + [sub + i * PEER_TOPK for i in range(1, half)] + [(sub + half) * PEER_TOPK]
    pos_iota = jnp.concatenate(pos_blocks, axis=0)
    idx_rows, gate_rows = [], []
    for h in range(PEER_HEADS):
        c0 = h * 2 * PEER_HALF
        v1, i1 = _top16(_dot_nt(k1_ref[...], qp_ref[:, c0:c0 + PEER_HALF]))
        v2, i2 = _top16(_dot_nt(k2_ref[...], qp_ref[:, c0 + PEER_HALF:c0 + 2 * PEER_HALF]))
        cand = jnp.concatenate([v1[0:1, :] + v2] + [v1[i:i + 1, :] + v2[:half, :] for i in range(1, half)]
                               + [v1[half:, :] + v2[0:1, :]], axis=0)
        vs, ids = [], []
        for _ in range(PEER_TOPK):
            m = jnp.max(cand, axis=0, keepdims=True)
            pos = jnp.min(jnp.where(cand == m, pos_iota, PEER_TOPK * PEER_TOPK), axis=0, keepdims=True)
            cand = jnp.where(pos_iota == pos, -jnp.inf, cand)
            vs.append(m)
            ids.append(_pick(i1, pos // PEER_TOPK) * N_KEYS + _pick(i2, pos % PEER_TOPK))
        vs = jnp.concatenate(vs, axis=0)
        e = jnp.exp(vs - vs[0:1, :])
        gate_rows.append(e / jnp.sum(e, axis=0, keepdims=True))
        idx_rows.append(jnp.concatenate(ids, axis=0))
    idx_ref[...] = jnp.concatenate(idx_rows, axis=0).T
    gate_ref[...] = jnp.concatenate(gate_rows, axis=0).T


def _retrieve(qp, sub_keys1, sub_keys2):
    t, dq = qp.shape
    out = pl.BlockSpec((TOPK_TILE, N_SEL), lambda i: (i, 0))
    return pl.pallas_call(
        _retrieve_kernel,
        grid=(t // TOPK_TILE,),
        in_specs=[pl.BlockSpec((TOPK_TILE, dq), lambda i: (i, 0)), _full((N_KEYS, PEER_HALF)),
                  _full((N_KEYS, PEER_HALF))],
        out_specs=[out, out],
        out_shape=[jax.ShapeDtypeStruct((t, N_SEL), jnp.int32), jax.ShapeDtypeStruct((t, N_SEL), _F32)],
        compiler_params=_params(),
        name="peer_retrieve",
    )(qp, sub_keys1.astype(_BF16), sub_keys2.astype(_BF16))


def _pack_kernel(u_ref, v_ref, o_ref):
    d = u_ref.shape[1]
    o_ref[:, 0, :d] = u_ref[...]
    o_ref[:, 0, d:] = v_ref[...]


def _pack_experts(expert_u, expert_v):
    n, d = expert_u.shape
    spec = pl.BlockSpec((PACK_TILE, d), lambda i: (i, 0))
    return pl.pallas_call(
        _pack_kernel,
        grid=(n // PACK_TILE,),
        in_specs=[spec, spec],
        out_specs=pl.BlockSpec((PACK_TILE, 1, 2 * d), lambda i: (i, 0, 0)),
        out_shape=jax.ShapeDtypeStruct((n, 1, 2 * d), _F32),
        compiler_params=_params(),
        name="pack_experts",
    )(expert_u, expert_v)


def _expert_kernel(idx_hbm, table_hbm, xn_ref, gates_ref, h_ref, o_ref,
                   idx_smem, rows, idx_sems, row_sems, *, n_steps):
    step = pl.program_id(0)
    cur = step % 2
    last = step + 1 == n_steps
    tile_idx = EXP_TILE * N_SEL
    d = D_MODEL
    per_piece = N_SEL // EXP_PIECES
    chunk = 2 * d // EXP_PIECES

    def idx_copy(s, half):
        return pltpu.make_async_copy(idx_hbm.at[s], idx_smem.at[pl.ds(half * tile_idx, tile_idx)],
                                     idx_sems.at[half])

    def issue(base, buf, piece):
        for k in range(piece * per_piece, (piece + 1) * per_piece):
            pltpu.make_async_copy(table_hbm.at[idx_smem[base + k]], rows.at[buf, pl.ds(k, 1)],
                                  row_sems.at[buf]).start(priority=k % 2)

    def wait_rows(buf):
        pltpu.make_async_copy(rows.at[buf], rows.at[buf], row_sems.at[buf]).wait()

    own_row = lax.broadcasted_iota(jnp.int32, (EXP_RING, N_SEL), 0)
    half_pieces = EXP_PIECES // 2

    def u_piece(j, c, xg, pre):
        return pre + _dot_nt(xg[:, c * chunk:(c + 1) * chunk],
                             rows[j, :, pl.ds(c * chunk, chunk)].astype(_BF16))

    def v_piece(j, c, a, acc_c):
        return acc_c + _dot(a, rows[j, :, pl.ds(d + c * chunk, chunk)].astype(_BF16))

    @pl.when(step == 0)
    def _():
        first = idx_copy(0, 0)
        first.start()
        first.wait()

        def prologue(j, carry):
            for piece in range(EXP_PIECES):
                issue(j * N_SEL, j, piece)
            return carry

        lax.fori_loop(0, EXP_AHEAD, prologue, 0)

    @pl.when(jnp.logical_not(last))
    def _():
        idx_copy(step + 1, 1 - cur).start()

    next_idx_at = (EXP_RING - EXP_AHEAD % EXP_RING) % EXP_RING

    def ring_body(i, carry):
        t0 = pl.multiple_of(i * EXP_RING, EXP_RING)
        grp = pl.ds(t0, EXP_RING)
        xg = xn_ref[grp, :]
        gates = gates_ref[grp, :]
        h = h_ref[grp, :]
        acc = [h[:, c * chunk:(c + 1) * chunk] for c in range(half_pieces)]
        wait_rows(0)
        pre = jnp.zeros((EXP_RING, N_SEL), _F32)
        for c in range(half_pieces):
            pre = u_piece(0, c, xg, pre)
        for j in range(EXP_RING):
            ta = t0 + j + EXP_AHEAD
            if j == next_idx_at:
                @pl.when(jnp.logical_and(ta == EXP_TILE, jnp.logical_not(last)))
                def _():
                    idx_copy(step + 1, 1 - cur).wait()
            past = ta >= EXP_TILE
            half = jnp.where(jnp.logical_and(past, jnp.logical_not(last)), 1 - cur, cur)
            base_ahead = half * tile_idx + jnp.where(past, ta - EXP_TILE, ta) * N_SEL
            buf_ahead = (j + EXP_AHEAD) % EXP_RING
            more = j + 1 < EXP_RING
            if more:
                wait_rows(j + 1)
                nxt = jnp.zeros((EXP_RING, N_SEL), _F32)
                for c in range(half_pieces):
                    issue(base_ahead, buf_ahead, c)
                    nxt = u_piece(j + 1, c, xg, nxt)
            a = jnp.where(own_row == j, jax.nn.gelu(pre) * gates, 0.0).astype(_BF16)
            for c in range(half_pieces):
                if more:
                    issue(base_ahead, buf_ahead, half_pieces + c)
                else:
                    issue(base_ahead, buf_ahead, 2 * c)
                    issue(base_ahead, buf_ahead, 2 * c + 1)
                acc[c] = v_piece(j, c, a, acc[c])
            if more:
                pre = nxt
        o_ref[grp, :] = jnp.concatenate(acc, axis=1)
        return carry

    lax.fori_loop(0, EXP_TILE // EXP_RING, ring_body, 0)

    @pl.when(last)
    def _():
        for j in range(EXP_AHEAD):
            wait_rows(j)


def _expert_stage(h, hn, idx, gates, table):
    t, d = h.shape
    n_steps = t // EXP_TILE
    tok = pl.BlockSpec((EXP_TILE, d), lambda i: (i, 0))
    return pl.pallas_call(
        functools.partial(_expert_kernel, n_steps=n_steps),
        grid=(n_steps,),
        in_specs=[pl.BlockSpec(memory_space=pl.ANY), pl.BlockSpec(memory_space=pl.ANY), tok,
                  pl.BlockSpec((EXP_TILE, N_SEL), lambda i: (i, 0)), tok],
        out_specs=tok,
        out_shape=jax.ShapeDtypeStruct((t, d), _F32),
        scratch_shapes=[pltpu.SMEM((2 * EXP_TILE * N_SEL,), jnp.int32),
                        pltpu.VMEM((EXP_RING, N_SEL, 2 * d), _F32),
                        pltpu.SemaphoreType.DMA((2,)),
                        pltpu.SemaphoreType.DMA((EXP_RING,))],
        compiler_params=_params(),
        name="peer_experts",
    )(idx.reshape(n_steps, EXP_TILE * N_SEL), table, hn, gates, h)


def _ple_kernel(h_ref, p_ref, g_ref, wg_ref, wp_ref, o_ref):
    h = h_ref[...]
    gate = jax.nn.sigmoid(_dot(_rmsnorm(h, g_ref[...]).astype(_BF16), wg_ref[...]))
    o_ref[...] = h + gate * _dot(p_ref[...].astype(_BF16), wp_ref[...])


def _ple(h, p, ln_pl, w_gate, w_proj):
    t = h.shape[0]
    row = lambda d: pl.BlockSpec((ROW_TILE, d), lambda i: (i, 0))
    return pl.pallas_call(
        _ple_kernel,
        grid=(t // ROW_TILE,),
        in_specs=[row(D_MODEL), row(D_PLE), _full((1, D_MODEL)), _full((D_MODEL, D_MODEL)),
                  _full((D_PLE, D_MODEL))],
        out_specs=row(D_MODEL),
        out_shape=jax.ShapeDtypeStruct((t, D_MODEL), _F32),
        compiler_params=_params(),
        name="ple",
    )(h, p, ln_pl.reshape(1, D_MODEL), w_gate.astype(_BF16), w_proj.astype(_BF16))


def kernel(x, p, ln_mix, w_in, b_fgate, q_norm, k_norm, conv_w, conv_b, conv_ln_g, conv_ln_b,
           attn_out_norm, conv_out_norm, w_out, ln_ffn, w_peer_q, sub_keys1, sub_keys2,
           expert_u, expert_v, ln_pl, w_pl_gate, w_pl_proj):
    b, s, _ = x.shape
    t = b * s
    h = x.reshape(t, D_MODEL)
    for i in range(ln_mix.shape[0]):
        qa, ka, va, u = _mix_in(h, s, ln_mix[i], w_in[i], b_fgate[i], q_norm[i], k_norm[i])
        y_attn = _attention(qa, ka, va, b, s)
        y_conv = _conv(u, s, conv_w[i], conv_b[i], conv_ln_g[i], conv_ln_b[i], conv_out_norm[i])
        h, hn, qp = _mix_out(y_attn, y_conv, h, attn_out_norm[i], w_out[i], ln_ffn[i], w_peer_q[i])
        idx, gates = _retrieve(qp, sub_keys1[i], sub_keys2[i])
        table = _pack_experts(expert_u[i], expert_v[i])
        h = _expert_stage(h, hn, idx, gates, table)
        h = _ple(h, p[i].reshape(t, D_PLE), ln_pl[i], w_pl_gate[i], w_pl_proj[i])
    return h.reshape(b, s, D_MODEL)
```

```python
import functools

import jax
import jax.numpy as jnp
import numpy as np
from jax import lax
from jax.experimental import pallas as pl
from jax.experimental.pallas import tpu as pltpu

D_MODEL = 1024
D_ATTN = 512
N_HEADS = 8
HEAD_DIM = 64
HEAD_PAD = 128
D_CONV = 512
CONV_WIDTH = 31
CONV_HALO = 32
N_KEYS = 128
PEER_HEADS = 8
PEER_HALF = 128
PEER_TOPK = 16
N_SEL = PEER_HEADS * PEER_TOPK
D_PLE = 256
EPS = 1e-6
LN_EPS = 1e-5
NEG_BIG = -1e30

ROW_TILE = 512
ATTN_BLOCK = 256
ATTN_HEADS = 4
CONV_CHUNK = 64
TOPK_TILE = 128
PACK_TILE = 512
EXP_TILE = 128
EXP_RING = 16
EXP_AHEAD = 14
EXP_PIECES = 8
VMEM_LIMIT_BYTES = 56 * 1024 * 1024

_F32 = jnp.float32
_BF16 = jnp.bfloat16


def _params(n_axes=1):
    return pltpu.CompilerParams(dimension_semantics=("arbitrary",) * n_axes,
                                vmem_limit_bytes=VMEM_LIMIT_BYTES)


def _full(shape):
    return pl.BlockSpec(shape, lambda *_: (0,) * len(shape))


def _dot(a, b):
    return jnp.dot(a, b, preferred_element_type=_F32)


def _dot_nt(a, b):
    return lax.dot_general(a, b, (((1,), (1,)), ((), ())), preferred_element_type=_F32)


def _split3(x):
    hi = x.astype(_BF16)
    r1 = x - hi.astype(_F32)
    mid = r1.astype(_BF16)
    lo = (r1 - mid.astype(_F32)).astype(_BF16)
    return hi, mid, lo


def _rmsnorm(x, g, eps=EPS):
    return x * lax.rsqrt(jnp.mean(x * x, axis=-1, keepdims=True) + eps) * g


def _mix_in_kernel(x_ref, g_ref, wq_ref, wk_ref, wv_ref, wf_ref, bf_ref, wa_ref, wg_ref, bd_ref, tri_ref,
                   pq_ref, pk_ref, qg_ref, kg_ref, qone_ref, kone_ref, vone_ref,
                   qa_ref, ka_ref, va_ref, u_ref, carry_ref, *, tiles_per_seq):
    @pl.when(pl.program_id(0) % tiles_per_seq == 0)
    def _():
        carry_ref[...] = jnp.zeros_like(carry_ref)

    xb = _rmsnorm(x_ref[...], g_ref[...]).astype(_BF16)

    def head_norm(z, gain):
        hi, mid, _ = _split3(z * z)
        ms = _dot(hi, bd_ref[...]) + _dot(mid, bd_ref[...])
        return z * lax.rsqrt(ms + EPS) * gain

    qn = head_norm(_dot(xb, wq_ref[...]), qg_ref[...])
    kn = head_norm(_dot(xb, wk_ref[...]), kg_ref[...])

    z = _dot(xb, wf_ref[...]) + bf_ref[...]
    logf = jnp.minimum(z, 0.0) - jnp.log1p(jnp.exp(-jnp.abs(z)))
    fsum = carry_ref[0:1, :]
    for part in _split3(logf):
        fsum = fsum + _dot(tri_ref[...], part)
    carry_ref[...] = jnp.broadcast_to(fsum[-1:, :], carry_ref.shape)

    f_q = qone_ref[...]
    f_k = kone_ref[...]
    for c, part in enumerate(_split3(fsum)):
        f_q = f_q + _dot(part, pq_ref[c])
        f_k = f_k - _dot(part, pk_ref[c])
    qa_ref[...] = (qn + f_q).astype(_BF16)
    ka_ref[...] = (kn + f_k).astype(_BF16)
    va_ref[...] = (_dot(xb, wv_ref[...]) + vone_ref[...]).astype(_BF16)
    u_ref[...] = _dot(xb, wa_ref[...]) * jax.nn.sigmoid(_dot(xb, wg_ref[...]))


def _pad_heads(w):
    lead = w.shape[:-1]
    w = w.reshape(*lead, N_HEADS, HEAD_DIM)
    w = jnp.pad(w, [(0, 0)] * len(lead) + [(0, 0), (0, HEAD_PAD - HEAD_DIM)])
    return w.reshape(*lead, N_HEADS * HEAD_PAD)


def _place(lane0):
    m = np.zeros((3, 128, N_HEADS * HEAD_PAD), np.float32)
    for c in range(3):
        for h in range(N_HEADS):
            m[c, h, h * HEAD_PAD + lane0 + c] = 1.0
    return jnp.asarray(m, _BF16)


def _lane_ones(lanes):
    m = np.zeros((1, N_HEADS * HEAD_PAD), np.float32)
    for h in range(N_HEADS):
        for l in lanes:
            m[0, h * HEAD_PAD + l] = 1.0
    return jnp.asarray(m)


def _mix_in(x, seq, ln_mix, w_in, b_fgate, q_norm, k_norm):
    t = x.shape[0]
    dp = N_HEADS * HEAD_PAD
    o = 0
    wq = _pad_heads(w_in[:, o:o + D_ATTN]).astype(_BF16); o += D_ATTN
    wk = _pad_heads(w_in[:, o:o + D_ATTN]).astype(_BF16); o += D_ATTN
    wv = _pad_heads(w_in[:, o:o + D_ATTN]).astype(_BF16); o += D_ATTN
    wf = jnp.pad(w_in[:, o:o + N_HEADS], ((0, 0), (0, 128 - N_HEADS))).astype(_BF16); o += N_HEADS
    wa = w_in[:, o:o + D_CONV].astype(_BF16); o += D_CONV
    wg = w_in[:, o:o + D_CONV].astype(_BF16)
    bf = jnp.pad(b_fgate, (0, 128 - N_HEADS)).reshape(1, 128)
    head_of = np.arange(dp) // HEAD_PAD
    bd = jnp.asarray((head_of[:, None] == head_of[None, :]) / HEAD_DIM, _BF16)
    tri = jnp.asarray(np.tril(np.ones((ROW_TILE, ROW_TILE), np.float32)), _BF16)
    qg = _pad_heads(jnp.tile(q_norm, N_HEADS) * HEAD_DIM ** -0.5).reshape(1, dp)
    kg = _pad_heads(jnp.tile(k_norm, N_HEADS)).reshape(1, dp)
    row = pl.BlockSpec((ROW_TILE, dp), lambda i: (i, 0))
    return pl.pallas_call(
        functools.partial(_mix_in_kernel, tiles_per_seq=seq // ROW_TILE),
        grid=(t // ROW_TILE,),
        in_specs=[pl.BlockSpec((ROW_TILE, D_MODEL), lambda i: (i, 0)), _full((1, D_MODEL)),
                  _full((D_MODEL, dp)), _full((D_MODEL, dp)), _full((D_MODEL, dp)),
                  _full((D_MODEL, 128)), _full((1, 128)), _full((D_MODEL, D_CONV)), _full((D_MODEL, D_CONV)),
                  _full((dp, dp)), _full((ROW_TILE, ROW_TILE)), _full((3, 128, dp)), _full((3, 128, dp)),
                  _full((1, dp)), _full((1, dp)), _full((1, dp)), _full((1, dp)), _full((1, dp))],
        out_specs=[row, row, row, pl.BlockSpec((ROW_TILE, D_CONV), lambda i: (i, 0))],
        out_shape=[jax.ShapeDtypeStruct((t, dp), _BF16)] * 3 + [jax.ShapeDtypeStruct((t, D_CONV), _F32)],
        scratch_shapes=[pltpu.VMEM((8, 128), _F32)],
        compiler_params=_params(),
        name="mix_in",
    )(x, ln_mix.reshape(1, D_MODEL), wq, wk, wv, wf, bf, wa, wg, bd, tri,
      _place(HEAD_DIM), _place(HEAD_DIM + 3), qg, kg,
      _lane_ones(range(HEAD_DIM + 3, HEAD_DIM + 6)), _lane_ones(range(HEAD_DIM, HEAD_DIM + 3)),
      _lane_ones([HEAD_DIM]))


def _attn_kernel(q_ref, k_ref, v_ref, o_ref):
    qi = pl.program_id(2)
    rows = lax.broadcasted_iota(jnp.int32, (ATTN_BLOCK, ATTN_BLOCK), 0)
    cols = lax.broadcasted_iota(jnp.int32, (ATTN_BLOCK, ATTN_BLOCK), 1)
    heads = [pl.ds(g * HEAD_PAD, HEAD_PAD) for g in range(ATTN_HEADS)]

    def block(kb, carry, diagonal):
        ks = pl.ds(pl.multiple_of(kb * ATTN_BLOCK, ATTN_BLOCK), ATTN_BLOCK)
        out = []
        for hd, (m, acc) in zip(heads, carry):
            s = _dot_nt(q_ref[:, hd], k_ref[ks, hd])
            if diagonal:
                s = jnp.where(rows >= cols, s, NEG_BIG)
            m_new = jnp.maximum(m, jnp.max(s, axis=1, keepdims=True))
            p = jnp.exp(s - m_new)
            out.append((m_new, acc * jnp.exp(m - m_new) + _dot(p.astype(_BF16), v_ref[ks, hd])))
        return tuple(out)

    init = tuple((jnp.full((ATTN_BLOCK, 1), NEG_BIG, _F32), jnp.zeros((ATTN_BLOCK, HEAD_PAD), _F32))
                 for _ in heads)
    carry = lax.fori_loop(0, qi, lambda kb, c: block(kb, c, False), init)
    lane = lax.broadcasted_iota(jnp.int32, (ATTN_BLOCK, HEAD_PAD), 1)
    for hd, (_, acc) in zip(heads, block(qi, carry, True)):
        o_ref[:, hd] = jnp.where(lane < HEAD_DIM, acc / acc[:, HEAD_DIM:HEAD_DIM + 1], 0.0)


def _attention(qa, ka, va, batch, seq):
    dp = N_HEADS * HEAD_PAD
    gw = ATTN_HEADS * HEAD_PAD
    qa, ka, va = (a.reshape(batch, seq, dp) for a in (qa, ka, va))
    qspec = pl.BlockSpec((None, ATTN_BLOCK, gw), lambda b, h, i: (b, i, h))
    kspec = pl.BlockSpec((None, seq, gw), lambda b, h, i: (b, 0, h))
    out = pl.pallas_call(
        _attn_kernel,
        grid=(batch, N_HEADS // ATTN_HEADS, seq // ATTN_BLOCK),
        in_specs=[qspec, kspec, kspec],
        out_specs=qspec,
        out_shape=jax.ShapeDtypeStruct((batch, seq, dp), _F32),
        compiler_params=_params(3),
        name="fox_attention",
    )(qa, ka, va)
    return out.reshape(batch * seq, dp)


def _conv_kernel(u_ref, halo_ref, w_ref, b_ref, lng_ref, lnb_ref, og_ref, o_ref, pad_ref, *, tiles_per_seq):
    first = pl.program_id(0) % tiles_per_seq == 0
    pad_ref[0:CONV_HALO, :] = jnp.where(first, 0.0, halo_ref[...])
    pad_ref[CONV_HALO:, :] = u_ref[...]
    lead = CONV_HALO - (CONV_WIDTH - 1)
    for c in range(ROW_TILE // CONV_CHUNK):
        r0 = c * CONV_CHUNK
        acc = jnp.zeros((CONV_CHUNK, D_CONV), _F32)
        for j in range(CONV_WIDTH):
            acc = acc + w_ref[j:j + 1, :] * pad_ref[r0 + lead + j:r0 + lead + j + CONV_CHUNK, :]
        y = acc + b_ref[...]
        mu = jnp.mean(y, axis=-1, keepdims=True)
        var = jnp.mean(jnp.square(y - mu), axis=-1, keepdims=True)
        y = (y - mu) * lax.rsqrt(var + LN_EPS) * lng_ref[...] + lnb_ref[...]
        y = y * jax.nn.sigmoid(y)
        o_ref[r0:r0 + CONV_CHUNK, :] = _rmsnorm(y, og_ref[...]).astype(_BF16)


def _conv(u, seq, conv_w, conv_b, ln_g, ln_b, out_g):
    t = u.shape[0]
    w = jnp.pad(conv_w.reshape(CONV_WIDTH, D_CONV), ((0, 32 - CONV_WIDTH), (0, 0)))
    per = ROW_TILE // CONV_HALO
    vec = lambda a: a.reshape(1, D_CONV)
    return pl.pallas_call(
        functools.partial(_conv_kernel, tiles_per_seq=seq // ROW_TILE),
        grid=(t // ROW_TILE,),
        in_specs=[pl.BlockSpec((ROW_TILE, D_CONV), lambda i: (i, 0)),
                  pl.BlockSpec((CONV_HALO, D_CONV), lambda i: (jnp.maximum(i * per - 1, 0), 0)),
                  _full((32, D_CONV))] + [_full((1, D_CONV))] * 4,
        out_specs=pl.BlockSpec((ROW_TILE, D_CONV), lambda i: (i, 0)),
        out_shape=jax.ShapeDtypeStruct((t, D_CONV), _BF16),
        scratch_shapes=[pltpu.VMEM((ROW_TILE + CONV_HALO, D_CONV), _F32)],
        compiler_params=_params(),
        name="conformer_conv",
    )(u, u, w, vec(conv_b), vec(ln_g), vec(ln_b), vec(out_g))


def _mix_out_kernel(ya_ref, yc_ref, x_ref, ag_ref, woa_ref, woc_ref, fg_ref, wq_ref, h_ref, hn_ref, qp_ref):
    ya = ya_ref[...]
    ms = jnp.sum(ya * ya, axis=-1, keepdims=True) * (1.0 / D_ATTN)
    yan = (ya * lax.rsqrt(ms + EPS) * ag_ref[...]).astype(_BF16)
    h = x_ref[...] + _dot(yan, woa_ref[...]) + _dot(yc_ref[...], woc_ref[...])
    h_ref[...] = h
    hn = _rmsnorm(h, fg_ref[...]).astype(_BF16)
    hn_ref[...] = hn
    qp_ref[...] = _dot(hn, wq_ref[...]).astype(_BF16)


def _mix_out(y_attn, y_conv, x, attn_out_norm, w_out, ln_ffn, w_peer_q):
    t = x.shape[0]
    dp = N_HEADS * HEAD_PAD
    dq = w_peer_q.shape[1]
    woa = jnp.pad(w_out[:D_ATTN].reshape(N_HEADS, HEAD_DIM, D_MODEL),
                  ((0, 0), (0, HEAD_PAD - HEAD_DIM), (0, 0))).reshape(dp, D_MODEL).astype(_BF16)
    row = lambda d: pl.BlockSpec((ROW_TILE, d), lambda i: (i, 0))
    return pl.pallas_call(
        _mix_out_kernel,
        grid=(t // ROW_TILE,),
        in_specs=[row(dp), row(D_CONV), row(D_MODEL), _full((1, dp)), _full((dp, D_MODEL)),
                  _full((D_CONV, D_MODEL)), _full((1, D_MODEL)), _full((D_MODEL, dq))],
        out_specs=[row(D_MODEL), row(D_MODEL), row(dq)],
        out_shape=[jax.ShapeDtypeStruct((t, D_MODEL), _F32), jax.ShapeDtypeStruct((t, D_MODEL), _BF16),
                   jax.ShapeDtypeStruct((t, dq), _BF16)],
        compiler_params=_params(),
        name="mix_out",
    )(y_attn, y_conv, x, _pad_heads(attn_out_norm).reshape(1, dp), woa, w_out[D_ATTN:].astype(_BF16),
      ln_ffn.reshape(1, D_MODEL), w_peer_q.astype(_BF16))


def _merge_sort_pairs(n):
    pairs = []
    p = 1
    while p < n:
        k = p
        while k >= 1:
            for j in range(k % p, n - k, 2 * k):
                for i in range(min(k, n - j - k)):
                    if (i + j) // (2 * p) == (i + j + k) // (2 * p):
                        pairs.append((i + j, i + j + k))
            k //= 2
        p *= 2
    return pairs


def _top16(s):
    slabs = s.shape[0] // 8
    sub = lax.broadcasted_iota(jnp.int32, (8, s.shape[1]), 0)
    vals = [s[8 * g:8 * g + 8, :] for g in range(slabs)]
    idxs = [sub + 8 * g for g in range(slabs)]
    for a, b in _merge_sort_pairs(slabs):
        first = (vals[a] > vals[b]) | ((vals[a] == vals[b]) & (idxs[a] < idxs[b]))
        vals[a], vals[b] = jnp.where(first, vals[a], vals[b]), jnp.where(first, vals[b], vals[a])
        idxs[a], idxs[b] = jnp.where(first, idxs[a], idxs[b]), jnp.where(first, idxs[b], idxs[a])
    top_v, top_i = [], []
    for it in range(PEER_TOPK):
        m = jnp.max(vals[0], axis=0, keepdims=True)
        pos = jnp.min(jnp.where(vals[0] == m, idxs[0], s.shape[0]), axis=0, keepdims=True)
        won = idxs[0] == pos
        top_v.append(m)
        top_i.append(pos)
        for g in range(PEER_TOPK - 1 - it):
            vals[g] = jnp.where(won, vals[g + 1], vals[g])
            idxs[g] = jnp.where(won, idxs[g + 1], idxs[g])
    return jnp.concatenate(top_v, axis=0), jnp.concatenate(top_i, axis=0)


def _pick(table, pos):
    row = lax.broadcasted_iota(jnp.int32, table.shape, 0)
    return jnp.max(jnp.where(row == pos, table, -1), axis=0, keepdims=True)


def _retrieve_tile(qp_ref, k1_ref, k2_ref):
    half = PEER_TOPK // 2
    sub = lax.broadcasted_iota(jnp.int32, (half, TOPK_TILE), 0)
    pos_blocks = [sub, sub + half] + [sub + i * PEER_TOPK for i in range(1, half)] + [(sub + half) * PEER_TOPK]
    pos_iota = jnp.concatenate(pos_blocks, axis=0)
    idx_rows, gate_rows = [], []
    for h in range(PEER_HEADS):
        c0 = h * 2 * PEER_HALF
        v1, i1 = _top16(_dot_nt(k1_ref[...], qp_ref[:, c0:c0 + PEER_HALF]))
        v2, i2 = _top16(_dot_nt(k2_ref[...], qp_ref[:, c0 + PEER_HALF:c0 + 2 * PEER_HALF]))
        cand = jnp.concatenate([v1[0:1, :] + v2] + [v1[i:i + 1, :] + v2[:half, :] for i in range(1, half)]
                               + [v1[half:, :] + v2[0:1, :]], axis=0)
        vs, ids = [], []
        for _ in range(PEER_TOPK):
            m = jnp.max(cand, axis=0, keepdims=True)
            pos = jnp.min(jnp.where(cand == m, pos_iota, PEER_TOPK * PEER_TOPK), axis=0, keepdims=True)
            cand = jnp.where(pos_iota == pos, -jnp.inf, cand)
            vs.append(m)
            ids.append(_pick(i1, pos // PEER_TOPK) * N_KEYS + _pick(i2, pos % PEER_TOPK))
        vs = jnp.concatenate(vs, axis=0)
        e = jnp.exp(vs - vs[0:1, :])
        gate_rows.append(e / jnp.sum(e, axis=0, keepdims=True))
        idx_rows.append(jnp.concatenate(ids, axis=0))
    return jnp.concatenate(idx_rows, axis=0).T, jnp.concatenate(gate_rows, axis=0).T


def _pack_kernel(u_ref, v_ref, o_ref):
    d = u_ref.shape[1]
    o_ref[:, 0, :d] = u_ref[...]
    o_ref[:, 0, d:] = v_ref[...]


def _pack_experts(expert_u, expert_v):
    n, d = expert_u.shape
    spec = pl.BlockSpec((PACK_TILE, d), lambda i: (i, 0))
    return pl.pallas_call(
        _pack_kernel,
        grid=(n // PACK_TILE,),
        in_specs=[spec, spec],
        out_specs=pl.BlockSpec((PACK_TILE, 1, 2 * d), lambda i: (i, 0, 0)),
        out_shape=jax.ShapeDtypeStruct((n, 1, 2 * d), _F32),
        compiler_params=_params(),
        name="pack_experts",
    )(expert_u, expert_v)


def _expert_kernel(table_hbm, qp_ref, k1_ref, k2_ref, xn_ref, h_ref, o_ref,
                   idx_vmem, gates_vmem, idx_smem, rows, idx_sem, row_sems, *, n_tiles):
    step = pl.program_id(0)
    cur = (step + 1) % 2
    last = step == n_tiles
    d = D_MODEL
    per_piece = N_SEL // EXP_PIECES
    chunk = 2 * d // EXP_PIECES

    def issue(row, buf, piece):
        for k in range(piece * per_piece, (piece + 1) * per_piece):
            pltpu.make_async_copy(table_hbm.at[idx_smem[row, k]], rows.at[buf, pl.ds(k, 1)],
                                  row_sems.at[buf]).start(priority=k % 2)

    def wait_rows(buf):
        pltpu.make_async_copy(rows.at[buf], rows.at[buf], row_sems.at[buf]).wait()

    own_row = lax.broadcasted_iota(jnp.int32, (EXP_RING, N_SEL), 0)
    half_pieces = EXP_PIECES // 2

    def u_piece(j, c, xg, pre):
        return pre + _dot_nt(xg[:, c * chunk:(c + 1) * chunk],
                             rows[j, :, pl.ds(c * chunk, chunk)].astype(_BF16))

    def v_piece(j, c, a, acc_c):
        return acc_c + _dot(a, rows[j, :, pl.ds(d + c * chunk, chunk)].astype(_BF16))

    @pl.when(step < n_tiles)
    def _():
        idx, gates = _retrieve_tile(qp_ref, k1_ref, k2_ref)
        idx_vmem[...] = idx
        gates_vmem[step % 2] = gates
        to_smem = pltpu.make_async_copy(idx_vmem, idx_smem.at[pl.ds((step % 2) * EXP_TILE, EXP_TILE)], idx_sem)
        to_smem.start()
        to_smem.wait()

    @pl.when(step == 0)
    def _():
        def prologue(j, carry):
            for piece in range(EXP_PIECES):
                issue(j, j, piece)
            return carry

        lax.fori_loop(0, EXP_AHEAD, prologue, 0)

    def ring_body(i, carry):
        t0 = pl.multiple_of(i * EXP_RING, EXP_RING)
        grp = pl.ds(t0, EXP_RING)
        xg = xn_ref[grp, :]
        gates = gates_vmem[cur, grp, :]
        h = h_ref[grp, :]
        acc = [h[:, c * chunk:(c + 1) * chunk] for c in range(half_pieces)]
        wait_rows(0)
        pre = jnp.zeros((EXP_RING, N_SEL), _F32)
        for c in range(half_pieces):
            pre = u_piece(0, c, xg, pre)
        for j in range(EXP_RING):
            ta = t0 + j + EXP_AHEAD
            past = ta >= EXP_TILE
            half = jnp.where(jnp.logical_and(past, jnp.logical_not(last)), 1 - cur, cur)
            base_ahead = half * EXP_TILE + jnp.where(past, ta - EXP_TILE, ta)
            buf_ahead = (j + EXP_AHEAD) % EXP_RING
            more = j + 1 < EXP_RING
            if more:
                wait_rows(j + 1)
                nxt = jnp.zeros((EXP_RING, N_SEL), _F32)
                for c in range(half_pieces):
                    issue(base_ahead, buf_ahead, c)
                    nxt = u_piece(j + 1, c, xg, nxt)
            a = jnp.where(own_row == j, jax.nn.gelu(pre) * gates, 0.0).astype(_BF16)
            for c in range(half_pieces):
                if more:
                    issue(base_ahead, buf_ahead, half_pieces + c)
                else:
                    issue(base_ahead, buf_ahead, 2 * c)
                    issue(base_ahead, buf_ahead, 2 * c + 1)
                acc[c] = v_piece(j, c, a, acc[c])
            if more:
                pre = nxt
        o_ref[grp, :] = jnp.concatenate(acc, axis=1)
        return carry

    @pl.when(step > 0)
    def _():
        lax.fori_loop(0, EXP_TILE // EXP_RING, ring_body, 0)

    @pl.when(last)
    def _():
        for j in range(EXP_AHEAD):
            wait_rows(j)


def _expert_stage(h, hn, qp, sub_keys1, sub_keys2, table):
    t, d = h.shape
    dq = qp.shape[1]
    n_tiles = t // EXP_TILE
    tok = pl.BlockSpec((EXP_TILE, d), lambda s: (jnp.maximum(s - 1, 0), 0))
    return pl.pallas_call(
        functools.partial(_expert_kernel, n_tiles=n_tiles),
        grid=(n_tiles + 1,),
        in_specs=[pl.BlockSpec(memory_space=pl.ANY),
                  pl.BlockSpec((EXP_TILE, dq), lambda s: (jnp.minimum(s, n_tiles - 1), 0)),
                  _full((N_KEYS, PEER_HALF)), _full((N_KEYS, PEER_HALF)), tok, tok],
        out_specs=tok,
        out_shape=jax.ShapeDtypeStruct((t, d), _F32),
        scratch_shapes=[pltpu.VMEM((EXP_TILE, N_SEL), jnp.int32),
                        pltpu.VMEM((2, EXP_TILE, N_SEL), _F32),
                        pltpu.SMEM((2 * EXP_TILE, N_SEL), jnp.int32),
                        pltpu.VMEM((EXP_RING, N_SEL, 2 * d), _F32),
                        pltpu.SemaphoreType.DMA(()),
                        pltpu.SemaphoreType.DMA((EXP_RING,))],
        compiler_params=_params(),
        name="peer_experts",
    )(table, qp, sub_keys1.astype(_BF16), sub_keys2.astype(_BF16), hn, h)


def _ple_kernel(h_ref, p_ref, g_ref, wg_ref, wp_ref, o_ref):
    h = h_ref[...]
    gate = jax.nn.sigmoid(_dot(_rmsnorm(h, g_ref[...]).astype(_BF16), wg_ref[...]))
    o_ref[...] = h + gate * _dot(p_ref[...].astype(_BF16), wp_ref[...])


def _ple(h, p, ln_pl, w_gate, w_proj):
    t = h.shape[0]
    row = lambda d: pl.BlockSpec((ROW_TILE, d), lambda i: (i, 0))
    return pl.pallas_call(
        _ple_kernel,
        grid=(t // ROW_TILE,),
        in_specs=[row(D_MODEL), row(D_PLE), _full((1, D_MODEL)), _full((D_MODEL, D_MODEL)),
                  _full((D_PLE, D_MODEL))],
        out_specs=row(D_MODEL),
        out_shape=jax.ShapeDtypeStruct((t, D_MODEL), _F32),
        compiler_params=_params(),
        name="ple",
    )(h, p, ln_pl.reshape(1, D_MODEL), w_gate.astype(_BF16), w_proj.astype(_BF16))


def kernel(x, p, ln_mix, w_in, b_fgate, q_norm, k_norm, conv_w, conv_b, conv_ln_g, conv_ln_b,
           attn_out_norm, conv_out_norm, w_out, ln_ffn, w_peer_q, sub_keys1, sub_keys2,
           expert_u, expert_v, ln_pl, w_pl_gate, w_pl_proj):
    b, s, _ = x.shape
    t = b * s
    h = x.reshape(t, D_MODEL)
    for i in range(ln_mix.shape[0]):
        qa, ka, va, u = _mix_in(h, s, ln_mix[i], w_in[i], b_fgate[i], q_norm[i], k_norm[i])
        y_attn = _attention(qa, ka, va, b, s)
        y_conv = _conv(u, s, conv_w[i], conv_b[i], conv_ln_g[i], conv_ln_b[i], conv_out_norm[i])
        h, hn, qp = _mix_out(y_attn, y_conv, h, attn_out_norm[i], w_out[i], ln_ffn[i], w_peer_q[i])
        table = _pack_experts(expert_u[i], expert_v[i])
        h = _expert_stage(h, hn, qp, sub_keys1[i], sub_keys2[i], table)
        h = _ple(h, p[i].reshape(t, D_PLE), ln_pl[i], w_pl_gate[i], w_pl_proj[i])
    return h.reshape(b, s, D_MODEL)
```

```python
import functools

import jax
import jax.numpy as jnp
import numpy as np
from jax import lax
from jax.experimental import pallas as pl
from jax.experimental.pallas import tpu as pltpu

D_MODEL = 1024
D_ATTN = 512
N_HEADS = 8
HEAD_DIM = 64
HEAD_PAD = 128
D_CONV = 512
CONV_WIDTH = 31
CONV_HALO = 32
N_KEYS = 128
PEER_HEADS = 8
PEER_HALF = 128
PEER_TOPK = 16
N_SEL = PEER_HEADS * PEER_TOPK
D_PLE = 256
EPS = 1e-6
LN_EPS = 1e-5
NEG_BIG = -1e30

ROW_TILE = 512
ATTN_BLOCK = 256
ATTN_HEADS = 8
CONV_CHUNK = 64
TOPK_TILE = 128
PACK_TILE = 512
EXP_TILE = 128
EXP_RING = 16
EXP_AHEAD = 12
EXP_PIECES = 8
VMEM_LIMIT_BYTES = 56 * 1024 * 1024

_F32 = jnp.float32
_BF16 = jnp.bfloat16


def _params(n_axes=1):
    return pltpu.CompilerParams(dimension_semantics=("arbitrary",) * n_axes,
                                vmem_limit_bytes=VMEM_LIMIT_BYTES)


def _full(shape):
    return pl.BlockSpec(shape, lambda *_: (0,) * len(shape))


def _dot(a, b):
    return jnp.dot(a, b, preferred_element_type=_F32)


def _dot_nt(a, b):
    return lax.dot_general(a, b, (((1,), (1,)), ((), ())), preferred_element_type=_F32)


def _split3(x):
    hi = x.astype(_BF16)
    r1 = x - hi.astype(_F32)
    mid = r1.astype(_BF16)
    lo = (r1 - mid.astype(_F32)).astype(_BF16)
    return hi, mid, lo


def _rmsnorm(x, g, eps=EPS):
    return x * lax.rsqrt(jnp.mean(x * x, axis=-1, keepdims=True) + eps) * g


def _mix_in_kernel(x_ref, g_ref, wq_ref, wk_ref, wv_ref, wf_ref, bf_ref, wa_ref, wg_ref, bd_ref, tri_ref,
                   pq_ref, pk_ref, qg_ref, kg_ref, qone_ref, kone_ref, vone_ref,
                   qa_ref, ka_ref, va_ref, u_ref, carry_ref, *, tiles_per_seq):
    @pl.when(pl.program_id(0) % tiles_per_seq == 0)
    def _():
        carry_ref[...] = jnp.zeros_like(carry_ref)

    xb = _rmsnorm(x_ref[...], g_ref[...]).astype(_BF16)

    def head_norm(z, gain):
        hi, mid, _ = _split3(z * z)
        ms = []
        for g in range(N_HEADS):
            hd = slice(g * HEAD_PAD, (g + 1) * HEAD_PAD)
            ms.append(_dot(hi[:, hd], bd_ref[...]) + _dot(mid[:, hd], bd_ref[...]))
        return z * lax.rsqrt(jnp.concatenate(ms, axis=1) + EPS) * gain

    qn = head_norm(_dot(xb, wq_ref[...]), qg_ref[...])
    kn = head_norm(_dot(xb, wk_ref[...]), kg_ref[...])

    z = _dot(xb, wf_ref[...]) + bf_ref[...]
    logf = jnp.minimum(z, 0.0) - jnp.log1p(jnp.exp(-jnp.abs(z)))
    fsum = carry_ref[0:1, :]
    for part in _split3(logf):
        fsum = fsum + _dot(tri_ref[...], part)
    carry_ref[...] = jnp.broadcast_to(fsum[-1:, :], carry_ref.shape)

    f_q = qone_ref[...]
    f_k = kone_ref[...]
    for c, part in enumerate(_split3(fsum)):
        f_q = f_q + _dot(part, pq_ref[c])
        f_k = f_k - _dot(part, pk_ref[c])
    qa_ref[...] = (qn + f_q).astype(_BF16)
    ka_ref[...] = (kn + f_k).astype(_BF16)
    va_ref[...] = (_dot(xb, wv_ref[...]) + vone_ref[...]).astype(_BF16)
    u_ref[...] = _dot(xb, wa_ref[...]) * jax.nn.sigmoid(_dot(xb, wg_ref[...]))


def _pad_heads(w):
    lead = w.shape[:-1]
    w = w.reshape(*lead, N_HEADS, HEAD_DIM)
    w = jnp.pad(w, [(0, 0)] * len(lead) + [(0, 0), (0, HEAD_PAD - HEAD_DIM)])
    return w.reshape(*lead, N_HEADS * HEAD_PAD)


def _place(lane0):
    m = np.zeros((3, 128, N_HEADS * HEAD_PAD), np.float32)
    for c in range(3):
        for h in range(N_HEADS):
            m[c, h, h * HEAD_PAD + lane0 + c] = 1.0
    return jnp.asarray(m, _BF16)


def _lane_ones(lanes):
    m = np.zeros((1, N_HEADS * HEAD_PAD), np.float32)
    for h in range(N_HEADS):
        for l in lanes:
            m[0, h * HEAD_PAD + l] = 1.0
    return jnp.asarray(m)


def _mix_in(x, seq, ln_mix, w_in, b_fgate, q_norm, k_norm):
    t = x.shape[0]
    dp = N_HEADS * HEAD_PAD
    o = 0
    wq = _pad_heads(w_in[:, o:o + D_ATTN]).astype(_BF16); o += D_ATTN
    wk = _pad_heads(w_in[:, o:o + D_ATTN]).astype(_BF16); o += D_ATTN
    wv = _pad_heads(w_in[:, o:o + D_ATTN]).astype(_BF16); o += D_ATTN
    wf = jnp.pad(w_in[:, o:o + N_HEADS], ((0, 0), (0, 128 - N_HEADS))).astype(_BF16); o += N_HEADS
    wa = w_in[:, o:o + D_CONV].astype(_BF16); o += D_CONV
    wg = w_in[:, o:o + D_CONV].astype(_BF16)
    bf = jnp.pad(b_fgate, (0, 128 - N_HEADS)).reshape(1, 128)
    bd = jnp.full((HEAD_PAD, HEAD_PAD), 1.0 / HEAD_DIM, _BF16)
    tri = jnp.asarray(np.tril(np.ones((ROW_TILE, ROW_TILE), np.float32)), _BF16)
    qg = _pad_heads(jnp.tile(q_norm, N_HEADS) * HEAD_DIM ** -0.5).reshape(1, dp)
    kg = _pad_heads(jnp.tile(k_norm, N_HEADS)).reshape(1, dp)
    row = pl.BlockSpec((ROW_TILE, dp), lambda i: (i, 0))
    return pl.pallas_call(
        functools.partial(_mix_in_kernel, tiles_per_seq=seq // ROW_TILE),
        grid=(t // ROW_TILE,),
        in_specs=[pl.BlockSpec((ROW_TILE, D_MODEL), lambda i: (i, 0)), _full((1, D_MODEL)),
                  _full((D_MODEL, dp)), _full((D_MODEL, dp)), _full((D_MODEL, dp)),
                  _full((D_MODEL, 128)), _full((1, 128)), _full((D_MODEL, D_CONV)), _full((D_MODEL, D_CONV)),
                  _full((HEAD_PAD, HEAD_PAD)), _full((ROW_TILE, ROW_TILE)), _full((3, 128, dp)), _full((3, 128, dp)),
                  _full((1, dp)), _full((1, dp)), _full((1, dp)), _full((1, dp)), _full((1, dp))],
        out_specs=[row, row, row, pl.BlockSpec((ROW_TILE, D_CONV), lambda i: (i, 0))],
        out_shape=[jax.ShapeDtypeStruct((t, dp), _BF16)] * 3 + [jax.ShapeDtypeStruct((t, D_CONV), _F32)],
        scratch_shapes=[pltpu.VMEM((8, 128), _F32)],
        compiler_params=_params(),
        name="mix_in",
    )(x, ln_mix.reshape(1, D_MODEL), wq, wk, wv, wf, bf, wa, wg, bd, tri,
      _place(HEAD_DIM), _place(HEAD_DIM + 3), qg, kg,
      _lane_ones(range(HEAD_DIM + 3, HEAD_DIM + 6)), _lane_ones(range(HEAD_DIM, HEAD_DIM + 3)),
      _lane_ones([HEAD_DIM]))


def _attn_kernel(q_ref, k_ref, v_ref, o_ref):
    qi = pl.program_id(2)
    rows = lax.broadcasted_iota(jnp.int32, (ATTN_BLOCK, ATTN_BLOCK), 0)
    cols = lax.broadcasted_iota(jnp.int32, (ATTN_BLOCK, ATTN_BLOCK), 1)
    heads = [pl.ds(g * HEAD_PAD, HEAD_PAD) for g in range(ATTN_HEADS)]

    def block(kb, carry, diagonal):
        ks = pl.ds(pl.multiple_of(kb * ATTN_BLOCK, ATTN_BLOCK), ATTN_BLOCK)
        out = []
        for hd, (m, acc) in zip(heads, carry):
            s = _dot_nt(q_ref[:, hd], k_ref[ks, hd])
            if diagonal:
                s = jnp.where(rows >= cols, s, NEG_BIG)
            m_new = jnp.maximum(m, jnp.max(s, axis=1, keepdims=True))
            p = jnp.exp(s - m_new)
            out.append((m_new, acc * jnp.exp(m - m_new) + _dot(p.astype(_BF16), v_ref[ks, hd])))
        return tuple(out)

    init = tuple((jnp.full((ATTN_BLOCK, 1), NEG_BIG, _F32), jnp.zeros((ATTN_BLOCK, HEAD_PAD), _F32))
                 for _ in heads)
    carry = lax.fori_loop(0, qi, lambda kb, c: block(kb, c, False), init)
    lane = lax.broadcasted_iota(jnp.int32, (ATTN_BLOCK, HEAD_PAD), 1)
    for hd, (_, acc) in zip(heads, block(qi, carry, True)):
        o_ref[:, hd] = jnp.where(lane < HEAD_DIM, acc / acc[:, HEAD_DIM:HEAD_DIM + 1], 0.0)


def _attention(qa, ka, va, batch, seq):
    dp = N_HEADS * HEAD_PAD
    gw = ATTN_HEADS * HEAD_PAD
    qa, ka, va = (a.reshape(batch, seq, dp) for a in (qa, ka, va))
    qspec = pl.BlockSpec((None, ATTN_BLOCK, gw), lambda b, h, i: (b, i, h))
    kspec = pl.BlockSpec((None, seq, gw), lambda b, h, i: (b, 0, h))
    out = pl.pallas_call(
        _attn_kernel,
        grid=(batch, N_HEADS // ATTN_HEADS, seq // ATTN_BLOCK),
        in_specs=[qspec, kspec, kspec],
        out_specs=qspec,
        out_shape=jax.ShapeDtypeStruct((batch, seq, dp), _F32),
        compiler_params=_params(3),
        name="fox_attention",
    )(qa, ka, va)
    return out.reshape(batch * seq, dp)


def _conv_kernel(u_ref, halo_ref, w_ref, b_ref, lng_ref, lnb_ref, og_ref, o_ref, pad_ref, *, tiles_per_seq):
    first = pl.program_id(0) % tiles_per_seq == 0
    pad_ref[0:CONV_HALO, :] = jnp.where(first, 0.0, halo_ref[...])
    pad_ref[CONV_HALO:, :] = u_ref[...]
    lead = CONV_HALO - (CONV_WIDTH - 1)
    for c in range(ROW_TILE // CONV_CHUNK):
        r0 = c * CONV_CHUNK
        acc = jnp.zeros((CONV_CHUNK, D_CONV), _F32)
        for j in range(CONV_WIDTH):
            acc = acc + w_ref[j:j + 1, :] * pad_ref[r0 + lead + j:r0 + lead + j + CONV_CHUNK, :]
        y = acc + b_ref[...]
        mu = jnp.mean(y, axis=-1, keepdims=True)
        var = jnp.mean(jnp.square(y - mu), axis=-1, keepdims=True)
        y = (y - mu) * lax.rsqrt(var + LN_EPS) * lng_ref[...] + lnb_ref[...]
        y = y * jax.nn.sigmoid(y)
        o_ref[r0:r0 + CONV_CHUNK, :] = _rmsnorm(y, og_ref[...]).astype(_BF16)


def _conv(u, seq, conv_w, conv_b, ln_g, ln_b, out_g):
    t = u.shape[0]
    w = jnp.pad(conv_w.reshape(CONV_WIDTH, D_CONV), ((0, 32 - CONV_WIDTH), (0, 0)))
    per = ROW_TILE // CONV_HALO
    vec = lambda a: a.reshape(1, D_CONV)
    return pl.pallas_call(
        functools.partial(_conv_kernel, tiles_per_seq=seq // ROW_TILE),
        grid=(t // ROW_TILE,),
        in_specs=[pl.BlockSpec((ROW_TILE, D_CONV), lambda i: (i, 0)),
                  pl.BlockSpec((CONV_HALO, D_CONV), lambda i: (jnp.maximum(i * per - 1, 0), 0)),
                  _full((32, D_CONV))] + [_full((1, D_CONV))] * 4,
        out_specs=pl.BlockSpec((ROW_TILE, D_CONV), lambda i: (i, 0)),
        out_shape=jax.ShapeDtypeStruct((t, D_CONV), _BF16),
        scratch_shapes=[pltpu.VMEM((ROW_TILE + CONV_HALO, D_CONV), _F32)],
        compiler_params=_params(),
        name="conformer_conv",
    )(u, u, w, vec(conv_b), vec(ln_g), vec(ln_b), vec(out_g))


def _mix_out_kernel(ya_ref, yc_ref, x_ref, ag_ref, woa_ref, woc_ref, fg_ref, wq_ref, h_ref, hn_ref, qp_ref):
    ya = ya_ref[...]
    ms = jnp.sum(ya * ya, axis=-1, keepdims=True) * (1.0 / D_ATTN)
    yan = (ya * lax.rsqrt(ms + EPS) * ag_ref[...]).astype(_BF16)
    h = x_ref[...] + _dot(yan, woa_ref[...]) + _dot(yc_ref[...], woc_ref[...])
    h_ref[...] = h
    hn = _rmsnorm(h, fg_ref[...]).astype(_BF16)
    hn_ref[...] = hn
    qp_ref[...] = _dot(hn, wq_ref[...]).astype(_BF16)


def _mix_out(y_attn, y_conv, x, attn_out_norm, w_out, ln_ffn, w_peer_q):
    t = x.shape[0]
    dp = N_HEADS * HEAD_PAD
    dq = w_peer_q.shape[1]
    woa = jnp.pad(w_out[:D_ATTN].reshape(N_HEADS, HEAD_DIM, D_MODEL),
                  ((0, 0), (0, HEAD_PAD - HEAD_DIM), (0, 0))).reshape(dp, D_MODEL).astype(_BF16)
    row = lambda d: pl.BlockSpec((ROW_TILE, d), lambda i: (i, 0))
    return pl.pallas_call(
        _mix_out_kernel,
        grid=(t // ROW_TILE,),
        in_specs=[row(dp), row(D_CONV), row(D_MODEL), _full((1, dp)), _full((dp, D_MODEL)),
                  _full((D_CONV, D_MODEL)), _full((1, D_MODEL)), _full((D_MODEL, dq))],
        out_specs=[row(D_MODEL), row(D_MODEL), row(dq)],
        out_shape=[jax.ShapeDtypeStruct((t, D_MODEL), _F32), jax.ShapeDtypeStruct((t, D_MODEL), _BF16),
                   jax.ShapeDtypeStruct((t, dq), _BF16)],
        compiler_params=_params(),
        name="mix_out",
    )(y_attn, y_conv, x, _pad_heads(attn_out_norm).reshape(1, dp), woa, w_out[D_ATTN:].astype(_BF16),
      ln_ffn.reshape(1, D_MODEL), w_peer_q.astype(_BF16))


def _merge_sort_pairs(n):
    pairs = []
    p = 1
    while p < n:
        k = p
        while k >= 1:
            for j in range(k % p, n - k, 2 * k):
                for i in range(min(k, n - j - k)):
                    if (i + j) // (2 * p) == (i + j + k) // (2 * p):
                        pairs.append((i + j, i + j + k))
            k //= 2
        p *= 2
    return pairs


def _top16(s):
    slabs = s.shape[0] // 8
    sub = lax.broadcasted_iota(jnp.int32, (8, s.shape[1]), 0)
    vals = [s[8 * g:8 * g + 8, :] for g in range(slabs)]
    idxs = [sub + 8 * g for g in range(slabs)]
    for a, b in _merge_sort_pairs(slabs):
        first = (vals[a] > vals[b]) | ((vals[a] == vals[b]) & (idxs[a] < idxs[b]))
        vals[a], vals[b] = jnp.where(first, vals[a], vals[b]), jnp.where(first, vals[b], vals[a])
        idxs[a], idxs[b] = jnp.where(first, idxs[a], idxs[b]), jnp.where(first, idxs[b], idxs[a])
    top_v, top_i = [], []
    for it in range(PEER_TOPK):
        m = jnp.max(vals[0], axis=0, keepdims=True)
        pos = jnp.min(jnp.where(vals[0] == m, idxs[0], s.shape[0]), axis=0, keepdims=True)
        won = idxs[0] == pos
        top_v.append(m)
        top_i.append(pos)
        for g in range(PEER_TOPK - 1 - it):
            vals[g] = jnp.where(won, vals[g + 1], vals[g])
            idxs[g] = jnp.where(won, idxs[g + 1], idxs[g])
    return jnp.concatenate(top_v, axis=0), jnp.concatenate(top_i, axis=0)


def _pick(table, pos):
    row = lax.broadcasted_iota(jnp.int32, table.shape, 0)
    return jnp.max(jnp.where(row == pos, table, -1), axis=0, keepdims=True)


def _retrieve_kernel(qp_ref, k1_ref, k2_ref, idx_ref, gate_ref):
    half = PEER_TOPK // 2
    sub = lax.broadcasted_iota(jnp.int32, (half, TOPK_TILE), 0)
    pos_blocks = [sub, sub + half] + [sub + i * PEER_TOPK for i in range(1, half)] + [(sub + half) * PEER_TOPK]
    pos_iota = jnp.concatenate(pos_blocks, axis=0)
    idx_rows, gate_rows = [], []
    for h in range(PEER_HEADS):
        c0 = h * 2 * PEER_HALF
        v1, i1 = _top16(_dot_nt(k1_ref[...], qp_ref[:, c0:c0 + PEER_HALF]))
        v2, i2 = _top16(_dot_nt(k2_ref[...], qp_ref[:, c0 + PEER_HALF:c0 + 2 * PEER_HALF]))
        cand = jnp.concatenate([v1[0:1, :] + v2] + [v1[i:i + 1, :] + v2[:half, :] for i in range(1, half)]
                               + [v1[half:, :] + v2[0:1, :]], axis=0)
        vs, ids = [], []
        for _ in range(PEER_TOPK):
            m = jnp.max(cand, axis=0, keepdims=True)
            pos = jnp.min(jnp.where(cand == m, pos_iota, PEER_TOPK * PEER_TOPK), axis=0, keepdims=True)
            cand = jnp.where(pos_iota == pos, -jnp.inf, cand)
            vs.append(m)
            ids.append(_pick(i1, pos // PEER_TOPK) * N_KEYS + _pick(i2, pos % PEER_TOPK))
        vs = jnp.concatenate(vs, axis=0)
        e = jnp.exp(vs - vs[0:1, :])
        gate_rows.append(e / jnp.sum(e, axis=0, keepdims=True))
        idx_rows.append(jnp.concatenate(ids, axis=0))
    idx_ref[...] = jnp.concatenate(idx_rows, axis=0).T
    gate_ref[...] = jnp.concatenate(gate_rows, axis=0).T


def _retrieve(qp, sub_keys1, sub_keys2):
    t, dq = qp.shape
    out = pl.BlockSpec((TOPK_TILE, N_SEL), lambda i: (i, 0))
    return pl.pallas_call(
        _retrieve_kernel,
        grid=(t // TOPK_TILE,),
        in_specs=[pl.BlockSpec((TOPK_TILE, dq), lambda i: (i, 0)), _full((N_KEYS, PEER_HALF)),
                  _full((N_KEYS, PEER_HALF))],
        out_specs=[out, out],
        out_shape=[jax.ShapeDtypeStruct((t, N_SEL), jnp.int32), jax.ShapeDtypeStruct((t, N_SEL), _F32)],
        compiler_params=_params(),
        name="peer_retrieve",
    )(qp, sub_keys1.astype(_BF16), sub_keys2.astype(_BF16))


def _pack_kernel(u_ref, v_ref, o_ref):
    d = u_ref.shape[1]
    o_ref[:, 0, :d] = u_ref[...]
    o_ref[:, 0, d:] = v_ref[...]


def _pack_experts(expert_u, expert_v):
    n, d = expert_u.shape
    spec = pl.BlockSpec((PACK_TILE, d), lambda i: (i, 0))
    return pl.pallas_call(
        _pack_kernel,
        grid=(n // PACK_TILE,),
        in_specs=[spec, spec],
        out_specs=pl.BlockSpec((PACK_TILE, 1, 2 * d), lambda i: (i, 0, 0)),
        out_shape=jax.ShapeDtypeStruct((n, 1, 2 * d), _F32),
        compiler_params=_params(),
        name="pack_experts",
    )(expert_u, expert_v)


def _expert_kernel(idx_hbm, table_hbm, xn_ref, gates_ref, h_ref, o_ref,
                   idx_smem, rows, idx_sems, row_sems, *, n_steps):
    step = pl.program_id(0)
    cur = step % 2
    last = step + 1 == n_steps
    tile_idx = EXP_TILE * N_SEL
    d = D_MODEL
    per_piece = N_SEL // EXP_PIECES
    chunk = 2 * d // EXP_PIECES

    def idx_copy(s, half):
        return pltpu.make_async_copy(idx_hbm.at[s], idx_smem.at[pl.ds(half * tile_idx, tile_idx)],
                                     idx_sems.at[half])

    def issue(base, buf, piece):
        for k in range(piece * per_piece, (piece + 1) * per_piece):
            pltpu.make_async_copy(table_hbm.at[idx_smem[base + k]], rows.at[buf, pl.ds(k, 1)],
                                  row_sems.at[buf]).start(priority=k % 2)

    def wait_rows(buf):
        pltpu.make_async_copy(rows.at[buf], rows.at[buf], row_sems.at[buf]).wait()

    own_row = lax.broadcasted_iota(jnp.int32, (EXP_RING, N_SEL), 0)
    half_pieces = EXP_PIECES // 2

    def u_piece(j, c, xg, pre):
        return pre + _dot_nt(xg[:, c * chunk:(c + 1) * chunk],
                             rows[j, :, pl.ds(c * chunk, chunk)].astype(_BF16))

    def v_piece(j, c, a, acc_c):
        return acc_c + _dot(a, rows[j, :, pl.ds(d + c * chunk, chunk)].astype(_BF16))

    @pl.when(step == 0)
    def _():
        first = idx_copy(0, 0)
        first.start()
        first.wait()

        def prologue(j, carry):
            for piece in range(EXP_PIECES):
                issue(j * N_SEL, j, piece)
            return carry

        lax.fori_loop(0, EXP_AHEAD, prologue, 0)

    @pl.when(jnp.logical_not(last))
    def _():
        idx_copy(step + 1, 1 - cur).start()

    next_idx_at = (EXP_RING - EXP_AHEAD % EXP_RING) % EXP_RING

    def ring_body(i, carry):
        t0 = pl.multiple_of(i * EXP_RING, EXP_RING)
        grp = pl.ds(t0, EXP_RING)
        xg = xn_ref[grp, :]
        gates = gates_ref[grp, :]
        h = h_ref[grp, :]
        acc = [h[:, c * chunk:(c + 1) * chunk] for c in range(half_pieces)]
        wait_rows(0)
        pre = jnp.zeros((EXP_RING, N_SEL), _F32)
        for c in range(half_pieces):
            pre = u_piece(0, c, xg, pre)
        for j in range(EXP_RING):
            ta = t0 + j + EXP_AHEAD
            if j == next_idx_at:
                @pl.when(jnp.logical_and(ta == EXP_TILE, jnp.logical_not(last)))
                def _():
                    idx_copy(step + 1, 1 - cur).wait()
            past = ta >= EXP_TILE
            half = jnp.where(jnp.logical_and(past, jnp.logical_not(last)), 1 - cur, cur)
            base_ahead = half * tile_idx + jnp.where(past, ta - EXP_TILE, ta) * N_SEL
            buf_ahead = (j + EXP_AHEAD) % EXP_RING
            more = j + 1 < EXP_RING
            if more:
                wait_rows(j + 1)
                nxt = jnp.zeros((EXP_RING, N_SEL), _F32)
                for c in range(half_pieces):
                    issue(base_ahead, buf_ahead, c)
                    nxt = u_piece(j + 1, c, xg, nxt)
            a = jnp.where(own_row == j, jax.nn.gelu(pre) * gates, 0.0).astype(_BF16)
            for c in range(half_pieces):
                if more:
                    issue(base_ahead, buf_ahead, half_pieces + c)
                else:
                    issue(base_ahead, buf_ahead, 2 * c)
                    issue(base_ahead, buf_ahead, 2 * c + 1)
                acc[c] = v_piece(j, c, a, acc[c])
            if more:
                pre = nxt
        o_ref[grp, :] = jnp.concatenate(acc, axis=1)
        return carry

    lax.fori_loop(0, EXP_TILE // EXP_RING, ring_body, 0)

    @pl.when(last)
    def _():
        for j in range(EXP_AHEAD):
            wait_rows(j)


def _expert_stage(h, hn, idx, gates, table):
    t, d = h.shape
    n_steps = t // EXP_TILE
    tok = pl.BlockSpec((EXP_TILE, d), lambda i: (i, 0))
    return pl.pallas_call(
        functools.partial(_expert_kernel, n_steps=n_steps),
        grid=(n_steps,),
        in_specs=[pl.BlockSpec(memory_space=pl.ANY), pl.BlockSpec(memory_space=pl.ANY), tok,
                  pl.BlockSpec((EXP_TILE, N_SEL), lambda i: (i, 0)), tok],
        out_specs=tok,
        out_shape=jax.ShapeDtypeStruct((t, d), _F32),
        scratch_shapes=[pltpu.SMEM((2 * EXP_TILE * N_SEL,), jnp.int32),
                        pltpu.VMEM((EXP_RING, N_SEL, 2 * d), _F32),
                        pltpu.SemaphoreType.DMA((2,)),
                        pltpu.SemaphoreType.DMA((EXP_RING,))],
        compiler_params=_params(),
        name="peer_experts",
    )(idx.reshape(n_steps, EXP_TILE * N_SEL), table, hn, gates, h)


def _ple_kernel(h_ref, p_ref, g_ref, wg_ref, wp_ref, o_ref):
    h = h_ref[...]
    gate = jax.nn.sigmoid(_dot(_rmsnorm(h, g_ref[...]).astype(_BF16), wg_ref[...]))
    o_ref[...] = h + gate * _dot(p_ref[...].astype(_BF16), wp_ref[...])


def _ple(h, p, ln_pl, w_gate, w_proj):
    t = h.shape[0]
    row = lambda d: pl.BlockSpec((ROW_TILE, d), lambda i: (i, 0))
    return pl.pallas_call(
        _ple_kernel,
        grid=(t // ROW_TILE,),
        in_specs=[row(D_MODEL), row(D_PLE), _full((1, D_MODEL)), _full((D_MODEL, D_MODEL)),
                  _full((D_PLE, D_MODEL))],
        out_specs=row(D_MODEL),
        out_shape=jax.ShapeDtypeStruct((t, D_MODEL), _F32),
        compiler_params=_params(),
        name="ple",
    )(h, p, ln_pl.reshape(1, D_MODEL), w_gate.astype(_BF16), w_proj.astype(_BF16))


def kernel(x, p, ln_mix, w_in, b_fgate, q_norm, k_norm, conv_w, conv_b, conv_ln_g, conv_ln_b,
           attn_out_norm, conv_out_norm, w_out, ln_ffn, w_peer_q, sub_keys1, sub_keys2,
           expert_u, expert_v, ln_pl, w_pl_gate, w_pl_proj):
    b, s, _ = x.shape
    t = b * s
    h = x.reshape(t, D_MODEL)
    for i in range(ln_mix.shape[0]):
        qa, ka, va, u = _mix_in(h, s, ln_mix[i], w_in[i], b_fgate[i], q_norm[i], k_norm[i])
        y_attn = _attention(qa, ka, va, b, s)
        y_conv = _conv(u, s, conv_w[i], conv_b[i], conv_ln_g[i], conv_ln_b[i], conv_out_norm[i])
        h, hn, qp = _mix_out(y_attn, y_conv, h, attn_out_norm[i], w_out[i], ln_ffn[i], w_peer_q[i])
        idx, gates = _retrieve(qp, sub_keys1[i], sub_keys2[i])
        table = _pack_experts(expert_u[i], expert_v[i])
        h = _expert_stage(h, hn, idx, gates, table)
        h = _ple(h, p[i].reshape(t, D_PLE), ln_pl[i], w_pl_gate[i], w_pl_proj[i])
    return h.reshape(b, s, D_MODEL)
```
